```python
import math
import jax, jax.numpy as jnp
from jax import lax
import numpy as np

D_MODEL = 2048
BATCH = 1
SEQ = 16384
DEPTH = 2

N_MIXERS = 2
N_CONV_LAYERS = (DEPTH + 1) // 2
N_ATTN_LAYERS = DEPTH // 2

CONV_WIDTH = 3

HEAD_DIM = 128
N_HEADS = D_MODEL // HEAD_DIM
Q_BLOCK = 128
FORGET_BIAS_INIT = 3.0
NEG_INF = -1e30

N_GROUPS = 4
EXPERTS_PER_GROUP = 8
N_EXPERTS = N_GROUPS * EXPERTS_PER_GROUP
TOP_K = 2
D_EXPERT = D_MODEL // 4
DISPATCH_BLOCK = 128

RMS_EPS = 1e-6

kernel_name = "hybrid_shortconv_fox_hmoe"


def rmsnorm(x, g):
    xf = x.astype(jnp.float32)
    y = xf * lax.rsqrt(jnp.mean(xf * xf, axis=-1, keepdims=True) + RMS_EPS)
    return (y * g.astype(jnp.float32)).astype(x.dtype)


def short_conv_mixer(h, w_in, conv_w, w_out):
    D = h.shape[-1]
    proj = jnp.einsum('bsd,de->bse', h, w_in)
    gate_out, gate_in, u = jnp.split(proj, 3, axis=-1)
    z = gate_in * u
    z = lax.conv_general_dilated(
        z, conv_w[:, None, :].astype(z.dtype), window_strides=(1,),
        padding=[(CONV_WIDTH - 1, 0)],
        dimension_numbers=('NWC', 'WIO', 'NWC'), feature_group_count=D)
    return jnp.einsum('bsd,de->bse', gate_out * z, w_out)


def forgetting_attention(h, w_in, b_f, w_out):
    B, S, D = h.shape
    proj = jnp.einsum('bsd,de->bse', h, w_in)
    q = proj[..., :D].reshape(B, S, N_HEADS, HEAD_DIM).transpose(0, 2, 1, 3)
    k = proj[..., D:2 * D].reshape(B, S, N_HEADS, HEAD_DIM).transpose(0, 2, 1, 3)
    v = proj[..., 2 * D:3 * D].reshape(B, S, N_HEADS, HEAD_DIM).transpose(0, 2, 1, 3)
    f_logit = proj[..., 3 * D:].astype(jnp.float32) + b_f.astype(jnp.float32)
    c = jnp.cumsum(jax.nn.log_sigmoid(f_logit), axis=1).transpose(0, 2, 1)

    n_qb = S // Q_BLOCK
    q_blocks = q.reshape(B, N_HEADS, n_qb, Q_BLOCK, HEAD_DIM).transpose(2, 0, 1, 3, 4)
    c_blocks = c.reshape(B, N_HEADS, n_qb, Q_BLOCK).transpose(2, 0, 1, 3)
    k_pos = jnp.arange(S)
    scale = 1.0 / math.sqrt(HEAD_DIM)

    def one_block(args):
        q_i, c_i, blk = args
        s = jnp.einsum('bhqd,bhkd->bhqk', q_i, k,
                       preferred_element_type=jnp.float32) * scale
        s = s + c_i[..., :, None] - c[:, :, None, :]
        q_pos = blk * Q_BLOCK + jnp.arange(Q_BLOCK)
        s = jnp.where(k_pos[None, :] <= q_pos[:, None], s, NEG_INF)
        p = jax.nn.softmax(s, axis=-1)
        return jnp.einsum('bhqk,bhkd->bhqd', p.astype(v.dtype), v)

    o = lax.map(one_block, (q_blocks, c_blocks, jnp.arange(n_qb)))
    o = o.transpose(1, 0, 3, 2, 4).reshape(B, S, D)
    return jnp.einsum('bsd,de->bse', o, w_out)


def hierarchical_moe(h, rg_w, rg_b, re_w, re_b, w_gate, w_up, w_down):
    B, S, D = h.shape
    N = B * S
    t = h.reshape(N, D)
    g_logits = jnp.einsum('nd,dg->ng', t, rg_w).astype(jnp.float32) + rg_b.astype(jnp.float32)
    g_prob = jax.nn.softmax(g_logits, axis=-1)
    _, g_sel = lax.top_k(g_logits, 1)
    p_g = jnp.take_along_axis(g_prob, g_sel, axis=1)
    e_logits = jnp.einsum('nd,gde->nge', t, re_w).astype(jnp.float32) + re_b.astype(jnp.float32)
    e_logits = jnp.take_along_axis(e_logits, g_sel[:, :, None], axis=1)[:, 0]
    top_v, top_i = lax.top_k(e_logits, TOP_K)
    gates = jax.nn.softmax(top_v, axis=-1) * p_g
    expert_idx = g_sel * EXPERTS_PER_GROUP + top_i

    A = N * TOP_K
    flat_e = expert_idx.reshape(A)
    flat_tok = jnp.repeat(jnp.arange(N), TOP_K)
    flat_w = gates.reshape(A)
    order = jnp.argsort(flat_e)
    sorted_e = flat_e[order]
    counts = jnp.bincount(flat_e, length=N_EXPERTS)
    starts = jnp.cumsum(counts) - counts
    padded = (counts + DISPATCH_BLOCK - 1) // DISPATCH_BLOCK * DISPATCH_BLOCK
    p_ends = jnp.cumsum(padded)
    p_starts = p_ends - padded
    dest = p_starts[sorted_e] + (jnp.arange(A) - starts[sorted_e])
    P = A + N_EXPERTS * DISPATCH_BLOCK
    n_blocks = P // DISPATCH_BLOCK
    slot_tok = jnp.full((P,), N, dtype=jnp.int32).at[dest].set(flat_tok[order].astype(jnp.int32))
    slot_w = jnp.zeros((P,), jnp.float32).at[dest].set(flat_w[order])
    block_e = jnp.minimum(
        jnp.searchsorted(p_ends, jnp.arange(n_blocks) * DISPATCH_BLOCK, side='right'),
        N_EXPERTS - 1)

    t_pad = jnp.concatenate([t, jnp.zeros((1, D), t.dtype)], axis=0)
    xs = t_pad[slot_tok].reshape(n_blocks, DISPATCH_BLOCK, D)

    def expert_block(args):
        xb, e = args
        hid = jax.nn.silu(xb @ w_gate[e]) * (xb @ w_up[e])
        return hid @ w_down[e]

    out = lax.map(expert_block, (xs, block_e)).reshape(P, D)
    y = jnp.zeros((N + 1, D), t.dtype).at[slot_tok].add(out * slot_w[:, None].astype(t.dtype))[:N]
    return y.reshape(B, S, D)


def setup_inputs(seed: int = 0) -> dict:
    key = jax.random.key(seed)
    ks = jax.random.split(key, 20)
    D = D_MODEL

    def nrm(k, shape, scale):
        return jax.random.normal(k, shape, jnp.float32) * scale

    return {
        "x": nrm(ks[0], (BATCH, SEQ, D), 1.0),
        "mix_norm": 1.0 + nrm(ks[1], (DEPTH, D), 0.02),
        "ffn_norm": 1.0 + nrm(ks[2], (DEPTH, D), 0.02),
        "final_norm": 1.0 + nrm(ks[3], (D,), 0.02),
        "conv_w_in": nrm(ks[4], (N_CONV_LAYERS, D, 3 * D), D ** -0.5),
        "conv_w": nrm(ks[5], (N_CONV_LAYERS, CONV_WIDTH, D), CONV_WIDTH ** -0.5),
        "conv_w_out": nrm(ks[6], (N_CONV_LAYERS, D, D), D ** -0.5),
        "attn_w_in": nrm(ks[7], (N_ATTN_LAYERS, D, 3 * D + N_HEADS), D ** -0.5),
        "attn_b_f": FORGET_BIAS_INIT + nrm(ks[8], (N_ATTN_LAYERS, N_HEADS), 1.0),
        "attn_w_out": nrm(ks[9], (N_ATTN_LAYERS, D, D), D ** -0.5),
        "router_group_w": nrm(ks[10], (DEPTH, D, N_GROUPS), D ** -0.5),
        "router_group_b": nrm(ks[11], (DEPTH, N_GROUPS), 0.01),
        "router_expert_w": nrm(ks[12], (DEPTH, N_GROUPS, D, EXPERTS_PER_GROUP), D ** -0.5),
        "router_expert_b": nrm(ks[13], (DEPTH, N_GROUPS, EXPERTS_PER_GROUP), 0.01),
        "w_gate": nrm(ks[14], (DEPTH, N_EXPERTS, D, D_EXPERT), D ** -0.5),
        "w_up": nrm(ks[15], (DEPTH, N_EXPERTS, D, D_EXPERT), D ** -0.5),
        "w_down": nrm(ks[16], (DEPTH, N_EXPERTS, D_EXPERT, D), D_EXPERT ** -0.5),
    }


def reference(x, mix_norm, ffn_norm, final_norm, conv_w_in, conv_w, conv_w_out,
              attn_w_in, attn_b_f, attn_w_out, router_group_w, router_group_b,
              router_expert_w, router_expert_b, w_gate, w_up, w_down):
    for i in range(DEPTH):
        j = i // N_MIXERS
        hn = rmsnorm(x, mix_norm[i])
        if i % N_MIXERS == 0:
            x = x + short_conv_mixer(hn, conv_w_in[j], conv_w[j], conv_w_out[j])
        else:
            x = x + forgetting_attention(hn, attn_w_in[j], attn_b_f[j], attn_w_out[j])
        hn = rmsnorm(x, ffn_norm[i])
        x = x + hierarchical_moe(hn, router_group_w[i], router_group_b[i],
                                 router_expert_w[i], router_expert_b[i],
                                 w_gate[i], w_up[i], w_down[i])
    return rmsnorm(x, final_norm)
```

```python
import functools
import math

import jax
import jax.numpy as jnp
from jax import lax
from jax.experimental import pallas as pl
from jax.experimental.pallas import tpu as pltpu

RMS_EPS = 1e-6
NEG_INF = -1e30
TOP_K = 2

V7X_LANES = 128
V7X_BF16_SUBLANES = 16
V7X_VMEM_BYTES = 64 * 1024 * 1024

_BF16 = jnp.bfloat16
_F32 = jnp.float32


def _tile(n, pref):
    t = min(n, pref)
    while n % t:
        t //= 2
    return t


def _params(semantics, vmem_bytes):
    limit = min(int(vmem_bytes), V7X_VMEM_BYTES - 4 * 1024 * 1024)
    return pltpu.CompilerParams(dimension_semantics=semantics, vmem_limit_bytes=limit)


def _nbytes(shape, dtype):
    return math.prod(shape) * jnp.dtype(dtype).itemsize


def _rmsnorm(x, g):
    ms = jnp.mean(x * x, axis=-1, keepdims=True)
    return x * lax.rsqrt(ms + RMS_EPS) * g


def _mm(a, b):
    return jnp.dot(a, b, preferred_element_type=_F32)


def _conv_inproj_kernel(x_ref, g_ref, wgo_ref, wgi_ref, wu_ref, go_ref, z_ref, hn_ref):
    @pl.when(pl.program_id(1) == 0)
    def _():
        hn_ref[...] = _rmsnorm(x_ref[...], g_ref[...]).astype(_BF16)

    hn = hn_ref[...]
    go_ref[...] = _mm(hn, wgo_ref[...].astype(_BF16)).astype(go_ref.dtype)
    gate_in = _mm(hn, wgi_ref[...].astype(_BF16))
    u = _mm(hn, wu_ref[...].astype(_BF16))
    z_ref[...] = (gate_in * u).astype(z_ref.dtype)


def _conv_inproj(x, g, w_in):
    s, d = x.shape
    tm, tn = _tile(s, 1024), _tile(d, 256)
    nj = d // tn
    vmem = (2 * _nbytes((tm, d), _F32) + _nbytes((tm, d), _BF16) + 6 * _nbytes((d, tn), _F32)
            + 3 * _nbytes((d, tn), _BF16) + 4 * _nbytes((tm, tn), _BF16) + 4 * _nbytes((tm, tn), _F32)
            + 2 * _nbytes((tm, d), _F32))
    return pl.pallas_call(
        _conv_inproj_kernel,
        out_shape=(jax.ShapeDtypeStruct((s, d), _BF16), jax.ShapeDtypeStruct((s, d), _BF16)),
        grid=(s // tm, nj),
        in_specs=[
            pl.BlockSpec((tm, d), lambda i, j: (i, 0)),
            pl.BlockSpec((1, d), lambda i, j: (0, 0)),
            pl.BlockSpec((d, tn), lambda i, j: (0, j)),
            pl.BlockSpec((d, tn), lambda i, j: (0, j + nj)),
            pl.BlockSpec((d, tn), lambda i, j: (0, j + 2 * nj)),
        ],
        out_specs=(pl.BlockSpec((tm, tn), lambda i, j: (i, j)),
                   pl.BlockSpec((tm, tn), lambda i, j: (i, j))),
        scratch_shapes=[pltpu.VMEM((tm, d), _BF16)],
        compiler_params=_params(("parallel", "arbitrary"), vmem),
        name="conv_inproj",
    )(x, g, w_in, w_in, w_in)


def _conv_outproj_kernel(go_ref, z_ref, halo_ref, cw_ref, w_ref, x_ref, o_ref, y_ref, *, chunk):
    i = pl.program_id(0)

    @pl.when(pl.program_id(1) == 0)
    def _():
        tm, d = y_ref.shape
        row = lax.broadcasted_iota(jnp.int32, (tm, chunk), 0)
        keep = (i > 0).astype(_F32)
        last = halo_ref.shape[0] - 1
        for c in range(d // chunk):
            cs = slice(c * chunk, (c + 1) * chunk)
            z = z_ref[:, cs].astype(_F32)
            halo = halo_ref[:, cs].astype(_F32) * keep
            h1 = halo[last:last + 1, :]
            h2 = halo[last - 1:last, :]
            z1 = jnp.where(row == 0, h1, pltpu.roll(z, 1, axis=0))
            z2 = jnp.where(row == 0, h2, jnp.where(row == 1, h1, pltpu.roll(z, 2, axis=0)))
            cw = cw_ref[:, cs]
            conv = cw[0:1, :] * z2 + cw[1:2, :] * z1 + cw[2:3, :] * z
            y_ref[:, cs] = (go_ref[:, cs].astype(_F32) * conv).astype(_BF16)

    o_ref[...] = x_ref[...] + _mm(y_ref[...], w_ref[...].astype(_BF16))


def _conv_outproj(go, z, conv_w, w_out, x):
    s, d = x.shape
    tm, tn = _tile(s, 1024), _tile(d, 512)
    hb = V7X_BF16_SUBLANES
    chunk = _tile(d, 512)
    vmem = (4 * _nbytes((tm, d), _BF16) + _nbytes((tm, d), _BF16) + 2 * _nbytes((d, tn), _F32)
            + _nbytes((d, tn), _BF16) + 6 * _nbytes((tm, tn), _F32) + 8 * _nbytes((tm, chunk), _F32))
    return pl.pallas_call(
        functools.partial(_conv_outproj_kernel, chunk=chunk),
        out_shape=jax.ShapeDtypeStruct((s, d), _F32),
        grid=(s // tm, d // tn),
        in_specs=[
            pl.BlockSpec((tm, d), lambda i, j: (i, 0)),
            pl.BlockSpec((tm, d), lambda i, j: (i, 0)),
            pl.BlockSpec((hb, d), lambda i, j: (jnp.maximum(i * (tm // hb) - 1, 0), 0)),
            pl.BlockSpec((conv_w.shape[0], d), lambda i, j: (0, 0)),
            pl.BlockSpec((d, tn), lambda i, j: (0, j)),
            pl.BlockSpec((tm, tn), lambda i, j: (i, j)),
        ],
        out_specs=pl.BlockSpec((tm, tn), lambda i, j: (i, j)),
        scratch_shapes=[pltpu.VMEM((tm, d), _BF16)],
        compiler_params=_params(("parallel", "arbitrary"), vmem),
        name="conv_outproj",
    )(go, z, z, conv_w, w_out, x)


def _attn_inproj_kernel(x_ref, g_ref, w_ref, wf_ref, bf_ref, qkv_ref, fl_ref, hn_ref):
    @pl.when(pl.program_id(1) == 0)
    def _():
        hn = _rmsnorm(x_ref[...], g_ref[...]).astype(_BF16)
        hn_ref[...] = hn
        fl_ref[...] = _mm(hn, wf_ref[...].astype(_BF16)) + bf_ref[...]

    res = _mm(hn_ref[...], w_ref[...].astype(_BF16)).astype(qkv_ref.dtype)
    hd = qkv_ref.shape[2]
    for h in range(qkv_ref.shape[0]):
        qkv_ref[h] = res[:, h * hd:(h + 1) * hd]


def _attn_inproj(x, g, w_in, wf, bf, n_heads):
    s, d = x.shape
    hd = d // n_heads
    tm, tn = _tile(s, 1024), _tile(d, 512)
    hpb = tn // hd
    vmem = (2 * _nbytes((tm, d), _F32) + _nbytes((tm, d), _BF16) + 2 * _nbytes((d, tn), _F32)
            + _nbytes((d, tn), _BF16) + 3 * _nbytes((tm, tn), _F32) + 2 * _nbytes((tm, d), _F32)
            + 4 * _nbytes((d, V7X_LANES), _F32))
    return pl.pallas_call(
        _attn_inproj_kernel,
        out_shape=(jax.ShapeDtypeStruct((3 * n_heads, s, hd), _BF16),
                   jax.ShapeDtypeStruct((s, V7X_LANES), _F32)),
        grid=(s // tm, 3 * d // tn),
        in_specs=[
            pl.BlockSpec((tm, d), lambda i, j: (i, 0)),
            pl.BlockSpec((1, d), lambda i, j: (0, 0)),
            pl.BlockSpec((d, tn), lambda i, j: (0, j)),
            pl.BlockSpec((d, V7X_LANES), lambda i, j: (0, 0)),
            pl.BlockSpec((1, V7X_LANES), lambda i, j: (0, 0)),
        ],
        out_specs=(pl.BlockSpec((hpb, tm, hd), lambda i, j: (j, i, 0)),
                   pl.BlockSpec((tm, V7X_LANES), lambda i, j: (i, 0))),
        scratch_shapes=[pltpu.VMEM((tm, d), _BF16)],
        compiler_params=_params(("parallel", "arbitrary"), vmem),
        name="attn_inproj",
    )(x, g, w_in, wf, bf)


def _forget_cumsum_kernel(fl_ref, c_ref, carry_ref):
    @pl.when(pl.program_id(0) == 0)
    def _():
        carry_ref[...] = jnp.zeros_like(carry_ref)

    fl = fl_ref[...]
    ls = jnp.minimum(fl, 0.0) - jnp.log1p(jnp.exp(-jnp.abs(fl)))
    tc = ls.shape[0]
    row = lax.broadcasted_iota(jnp.int32, ls.shape, 0)
    shift = 1
    while shift < tc:
        ls = jnp.where(row >= shift, ls + pltpu.roll(ls, shift, axis=0), ls)
        shift *= 2
    ls = ls + carry_ref[...]
    carry_ref[...] = ls[tc - 1:tc, :]
    c_ref[...] = ls.T


def _forget_cumsum(fl):
    s, lanes = fl.shape
    tc = _tile(s, 2048)
    return pl.pallas_call(
        _forget_cumsum_kernel,
        out_shape=jax.ShapeDtypeStruct((lanes, s), _F32),
        grid=(s // tc,),
        in_specs=[pl.BlockSpec((tc, lanes), lambda i: (i, 0))],
        out_specs=pl.BlockSpec((lanes, tc), lambda i: (0, i)),
        scratch_shapes=[pltpu.VMEM((1, lanes), _F32)],
        compiler_params=_params(("arbitrary",), 16 * _nbytes((tc, lanes), _F32)),
        name="forget_cumsum",
    )(fl)


def _fox_attn_kernel(q_ref, k_ref, v_ref, c_ref, o_ref, m_ref, l_ref, acc_ref, *, scale):
    i = pl.program_id(1)
    tq = q_ref.shape[0]
    q = q_ref[...]
    q0 = pl.multiple_of(i * tq, tq)
    c_base = c_ref[:, pl.ds(q0, tq)][:, 0:1]

    m_ref[...] = jnp.full_like(m_ref, -jnp.inf)
    l_ref[...] = jnp.zeros_like(l_ref)
    acc_ref[...] = jnp.zeros_like(acc_ref)

    def block(j, masked):
        k0 = pl.multiple_of(j * tq, tq)
        k = k_ref[pl.ds(k0, tq), :]
        v = v_ref[pl.ds(k0, tq), :]
        s = lax.dot_general(q, k, (((1,), (1,)), ((), ())), preferred_element_type=_F32) * scale
        s = s - (c_ref[:, pl.ds(k0, tq)] - c_base)
        if masked:
            qi = lax.broadcasted_iota(jnp.int32, s.shape, 0)
            ki = lax.broadcasted_iota(jnp.int32, s.shape, 1)
            s = jnp.where(ki <= qi, s, NEG_INF)
        m_prev = m_ref[...]
        m_new = jnp.maximum(m_prev, jnp.max(s, axis=-1, keepdims=True))
        alpha = jnp.exp(m_prev - m_new)
        p = jnp.exp(s - m_new)
        l_ref[...] = alpha * l_ref[...] + jnp.sum(p, axis=-1, keepdims=True)
        acc_ref[...] = alpha * acc_ref[...] + _mm(p.astype(_BF16), v)
        m_ref[...] = m_new

    def body(j, carry):
        block(j, False)
        return carry

    lax.fori_loop(0, i, body, 0)
    block(i, True)
    o_ref[...] = (acc_ref[...] / l_ref[...]).astype(o_ref.dtype)


def _fox_attention(qkv, c3, n_heads, d):
    _, s, hd = qkv.shape
    tq = _tile(s, 512)
    vmem = (8 * _nbytes((s, hd), _BF16) + 4 * _nbytes((1, s), _F32) + 8 * _nbytes((tq, hd), _F32)
            + 8 * _nbytes((tq, tq), _F32))
    return pl.pallas_call(
        functools.partial(_fox_attn_kernel, scale=1.0 / math.sqrt(hd)),
        out_shape=jax.ShapeDtypeStruct((s, d), _BF16),
        grid=(n_heads, s // tq),
        in_specs=[
            pl.BlockSpec((None, tq, hd), lambda h, i: (h, i, 0)),
            pl.BlockSpec((None, s, hd), lambda h, i: (n_heads + h, 0, 0)),
            pl.BlockSpec((None, s, hd), lambda h, i: (2 * n_heads + h, 0, 0)),
            pl.BlockSpec((None, 1, s), lambda h, i: (h, 0, 0)),
        ],
        out_specs=pl.BlockSpec((tq, hd), lambda h, i: (i, h)),
        scratch_shapes=[pltpu.VMEM((tq, 1), _F32), pltpu.VMEM((tq, 1), _F32), pltpu.VMEM((tq, hd), _F32)],
        compiler_params=_params(("parallel", "arbitrary"), vmem),
        name="fox_attention",
    )(qkv, qkv, qkv, c3)


def _matmul_residual_kernel(a_ref, w_ref, x_ref, o_ref):
    o_ref[...] = x_ref[...] + _mm(a_ref[...], w_ref[...].astype(_BF16))


def _matmul_residual(a, w, x):
    s, d = x.shape
    tm, tn = _tile(s, 1024), _tile(d, 512)
    vmem = (2 * _nbytes((tm, d), _BF16) + 2 * _nbytes((d, tn), _F32) + _nbytes((d, tn), _BF16)
            + 6 * _nbytes((tm, tn), _F32))
    return pl.pallas_call(
        _matmul_residual_kernel,
        out_shape=jax.ShapeDtypeStruct((s, d), _F32),
        grid=(s // tm, d // tn),
        in_specs=[
            pl.BlockSpec((tm, d), lambda i, j: (i, 0)),
            pl.BlockSpec((d, tn), lambda i, j: (0, j)),
            pl.BlockSpec((tm, tn), lambda i, j: (i, j)),
        ],
        out_specs=pl.BlockSpec((tm, tn), lambda i, j: (i, j)),
        compiler_params=_params(("parallel", "arbitrary"), vmem),
        name="attn_outproj",
    )(a, w, x)


def _router_kernel(x_ref, g_ref, wr_ref, br_ref, ids_ref, gates_ref, counts_ref, tril_ref, carry_ref,
                   *, n_groups, epg):
    i = pl.program_id(0)
    tm = x_ref.shape[0]
    lanes = wr_ref.shape[1]

    @pl.when(i == 0)
    def _():
        r = lax.broadcasted_iota(jnp.int32, (tm, tm), 0)
        c = lax.broadcasted_iota(jnp.int32, (tm, tm), 1)
        tril_ref[...] = (c < r).astype(_BF16)
        carry_ref[...] = jnp.zeros_like(carry_ref)

    hn = _rmsnorm(x_ref[...], g_ref[...])
    logits = _mm(hn.astype(_BF16), wr_ref[...].astype(_BF16)) + br_ref[...]
    lane = lax.broadcasted_iota(jnp.int32, (tm, lanes), 1)

    def first_argmax(vals):
        top = jnp.max(vals, axis=-1, keepdims=True)
        idx = jnp.min(jnp.where(vals == top, lane, lanes), axis=-1, keepdims=True)
        return top, idx

    gl = jnp.where(lane < n_groups, logits, -jnp.inf)
    g_top, g_sel = first_argmax(gl)
    p_g = 1.0 / jnp.sum(jnp.exp(gl - g_top), axis=-1, keepdims=True)
    lo = n_groups + g_sel * epg
    el = jnp.where((lane >= lo) & (lane < lo + epg), logits, -jnp.inf)
    v1, i1 = first_argmax(el)
    v2, i2 = first_argmax(jnp.where(lane == i1, -jnp.inf, el))
    t = jnp.exp(v2 - v1)
    den = 1.0 + t
    gate1 = (1.0 / den) * p_g
    gate2 = (t / den) * p_g

    sel1 = lane == i1
    sel2 = lane == i2
    onehot = (sel1 | sel2).astype(_F32)
    prefix = _mm(tril_ref[...], onehot.astype(_BF16)) + carry_ref[...]
    rank1 = jnp.sum(jnp.where(sel1, prefix, 0.0), axis=-1, keepdims=True).astype(jnp.int32)
    rank2 = jnp.sum(jnp.where(sel2, prefix, 0.0), axis=-1, keepdims=True).astype(jnp.int32)
    carry = carry_ref[...] + jnp.sum(onehot, axis=0, keepdims=True)
    carry_ref[...] = carry
    counts_ref[...] = carry

    ids = jnp.where(lane == 0, i1 - n_groups,
                    jnp.where(lane == 1, i2 - n_groups,
                              jnp.where(lane == 2, rank1, jnp.where(lane == 3, rank2, 0))))
    gates = jnp.where(lane == 0, gate1, jnp.where(lane == 1, gate2, 0.0))
    ids_ref[...] = ids[:, :ids_ref.shape[1]]
    gates_ref[...] = gates[:, :gates_ref.shape[1]]


def _router(x, g, wr, br, n_groups, epg):
    s, d = x.shape
    lanes = wr.shape[1]
    tm = _tile(s, 512)
    vmem = (4 * _nbytes((tm, d), _F32) + _nbytes((tm, tm), _BF16) + 4 * _nbytes((d, lanes), _F32)
            + 24 * _nbytes((tm, lanes), _F32) + 4 * _nbytes((tm, tm), jnp.int32))
    return pl.pallas_call(
        functools.partial(_router_kernel, n_groups=n_groups, epg=epg),
        out_shape=(jax.ShapeDtypeStruct((s, 8), jnp.int32), jax.ShapeDtypeStruct((s, 8), _F32),
                   jax.ShapeDtypeStruct((1, lanes), _F32)),
        grid=(s // tm,),
        in_specs=[
            pl.BlockSpec((tm, d), lambda i: (i, 0)),
            pl.BlockSpec((1, d), lambda i: (0, 0)),
            pl.BlockSpec((d, lanes), lambda i: (0, 0)),
            pl.BlockSpec((1, lanes), lambda i: (0, 0)),
        ],
        out_specs=(pl.BlockSpec((tm, 8), lambda i: (i, 0)), pl.BlockSpec((tm, 8), lambda i: (i, 0)),
                   pl.BlockSpec((1, lanes), lambda i: (0, 0))),
        scratch_shapes=[pltpu.VMEM((tm, tm), _BF16), pltpu.VMEM((1, lanes), _F32)],
        compiler_params=_params(("arbitrary",), vmem),
        name="moe_router",
    )(x, g, wr, br)


def _dispatch_kernel(dest_ref, x_ref, g_ref, xs_hbm, hn_ref, sem_ref):
    i = pl.program_id(0)
    n = pl.num_programs(0)
    tm = x_ref.shape[0]
    slot = i % 2

    def row_copy(s, r, dst):
        return pltpu.make_async_copy(hn_ref.at[s, pl.ds(r, 1), :], xs_hbm.at[pl.ds(dst, 1), :], sem_ref.at[s])

    def drain(s):
        def body(r, carry):
            row_copy(s, 0, 0).wait()
            row_copy(s, 0, 0).wait()
            return carry
        lax.fori_loop(0, tm, body, 0)

    hn_ref[slot] = _rmsnorm(x_ref[...], g_ref[...])

    def issue(r, carry):
        a = TOP_K * (i * tm + r)
        row_copy(slot, r, dest_ref[a]).start()
        row_copy(slot, r, dest_ref[a + 1]).start()
        return carry

    lax.fori_loop(0, tm, issue, 0)

    @pl.when(i > 0)
    def _():
        drain(1 - slot)

    @pl.when(i == n - 1)
    def _():
        drain(slot)


def _dispatch(dest, x, g):
    s, d = x.shape
    tm = _tile(s, 256)
    vmem = 6 * _nbytes((tm, d), _F32)
    return pl.pallas_call(
        _dispatch_kernel,
        out_shape=jax.ShapeDtypeStruct((TOP_K * s, d), _F32),
        grid_spec=pltpu.PrefetchScalarGridSpec(
            num_scalar_prefetch=1,
            grid=(s // tm,),
            in_specs=[pl.BlockSpec((tm, d), lambda i, dest: (i, 0)),
                      pl.BlockSpec((1, d), lambda i, dest: (0, 0))],
            out_specs=pl.BlockSpec(memory_space=pl.ANY),
            scratch_shapes=[pltpu.VMEM((2, tm, d), _F32), pltpu.SemaphoreType.DMA((2,))],
        ),
        compiler_params=_params(("arbitrary",), vmem),
        name="moe_dispatch",
    )(dest, x, g)


def _experts_kernel(wb_ref, we_ref, lo_ref, hi_ref, nw_ref, x_ref, wg_ref, wu_ref, wd_ref, o_ref,
                    wg_s, wu_s, wd_s, *, cast_rows):
    w = pl.program_id(0)

    @pl.when(w < nw_ref[0])
    def _():
        e = we_ref[w]
        b = wb_ref[w]
        prev = jnp.maximum(w - 1, 0)
        new_expert = (w == 0) | (we_ref[prev] != e)
        new_block = (w == 0) | (wb_ref[prev] != b)

        @pl.when(new_expert)
        def _():
            for src, dst in ((wg_ref, wg_s), (wu_ref, wu_s), (wd_ref, wd_s)):
                for r in range(0, src.shape[0], cast_rows):
                    dst[r:r + cast_rows, :] = src[r:r + cast_rows, :].astype(_BF16)

        tb = x_ref.shape[0]
        x = x_ref[...].astype(_BF16)
        gate = _mm(x, wg_s[...])
        up = _mm(x, wu_s[...])
        hid = gate * (1.0 / (1.0 + jnp.exp(-gate))) * up
        row = b * tb + lax.broadcasted_iota(jnp.int32, (tb, 1), 0)
        hid = jnp.where((row >= lo_ref[e]) & (row < hi_ref[e]), hid, 0.0).astype(_BF16)
        out = _mm(hid, wd_s[...])

        @pl.when(new_block)
        def _():
            o_ref[...] = out

        @pl.when(jnp.logical_not(new_block))
        def _():
            o_ref[...] += out


def _experts(plan, xs, w_gate, w_up, w_down, tb):
    a, d = xs.shape
    f = w_gate.shape[2]
    wb, we, lo, hi, nw = plan
    cast_rows = _tile(f, 256)
    vmem = (4 * _nbytes((tb, d), _F32) + 6 * _nbytes((d, f), _F32) + 3 * _nbytes((d, f), _BF16)
            + 6 * _nbytes((tb, f), _F32) + 3 * _nbytes((tb, d), _F32) + 2 * _nbytes((cast_rows, d), _F32))
    return pl.pallas_call(
        functools.partial(_experts_kernel, cast_rows=cast_rows),
        out_shape=jax.ShapeDtypeStruct((a, d), _F32),
        grid_spec=pltpu.PrefetchScalarGridSpec(
            num_scalar_prefetch=5,
            grid=(wb.shape[0],),
            in_specs=[
                pl.BlockSpec((tb, d), lambda w, wb, we, lo, hi, nw: (wb[w], 0)),
                pl.BlockSpec((None, d, f), lambda w, wb, we, lo, hi, nw: (we[w], 0, 0)),
                pl.BlockSpec((None, d, f), lambda w, wb, we, lo, hi, nw: (we[w], 0, 0)),
                pl.BlockSpec((None, f, d), lambda w, wb, we, lo, hi, nw: (we[w], 0, 0)),
            ],
            out_specs=pl.BlockSpec((tb, d), lambda w, wb, we, lo, hi, nw: (wb[w], 0)),
            scratch_shapes=[pltpu.VMEM((d, f), _BF16), pltpu.VMEM((d, f), _BF16), pltpu.VMEM((f, d), _BF16)],
        ),
        compiler_params=_params(("arbitrary",), vmem),
        name="moe_experts",
    )(wb, we, lo, hi, nw, xs, w_gate, w_up, w_down)


def _combine_kernel(dest_ref, x_ref, gates_ref, gf_ref, ys_hbm, o_ref, buf_ref, sem_ref, *, final_norm):
    i = pl.program_id(0)
    n = pl.num_programs(0)
    tm = x_ref.shape[0]
    slot = i % 2

    def row_copy(s, k, r, src):
        return pltpu.make_async_copy(ys_hbm.at[pl.ds(src, 1), :], buf_ref.at[s, k, pl.ds(r, 1), :],
                                     sem_ref.at[s])

    def issue(tile, s):
        def body(r, carry):
            a = TOP_K * (tile * tm + r)
            row_copy(s, 0, r, dest_ref[a]).start()
            row_copy(s, 1, r, dest_ref[a + 1]).start()
            return carry
        lax.fori_loop(0, tm, body, 0)

    @pl.when(i == 0)
    def _():
        issue(0, 0)

    @pl.when(i + 1 < n)
    def _():
        issue(i + 1, 1 - slot)

    def drain(r, carry):
        row_copy(slot, 0, 0, 0).wait()
        row_copy(slot, 1, 0, 0).wait()
        return carry

    lax.fori_loop(0, tm, drain, 0)

    gates = gates_ref[...]
    y = gates[:, 0:1] * buf_ref[slot, 0] + gates[:, 1:2] * buf_ref[slot, 1]
    out = x_ref[...] + y
    if final_norm:
        out = _rmsnorm(out, gf_ref[...])
    o_ref[...] = out


def _combine(dest, x, gates, ys, gf, final_norm):
    s, d = x.shape
    tm = _tile(s, 256)
    vmem = 4 * _nbytes((tm, d), _F32) + 4 * _nbytes((tm, d), _F32) + 6 * _nbytes((tm, d), _F32)
    return pl.pallas_call(
        functools.partial(_combine_kernel, final_norm=final_norm),
        out_shape=jax.ShapeDtypeStruct((s, d), _F32),
        grid_spec=pltpu.PrefetchScalarGridSpec(
            num_scalar_prefetch=1,
            grid=(s // tm,),
            in_specs=[
                pl.BlockSpec((tm, d), lambda i, dest: (i, 0)),
                pl.BlockSpec((tm, gates.shape[1]), lambda i, dest: (i, 0)),
                pl.BlockSpec((1, d), lambda i, dest: (0, 0)),
                pl.BlockSpec(memory_space=pl.ANY),
            ],
            out_specs=pl.BlockSpec((tm, d), lambda i, dest: (i, 0)),
            scratch_shapes=[pltpu.VMEM((2, TOP_K, tm, d), _F32), pltpu.SemaphoreType.DMA((2,))],
        ),
        compiler_params=_params(("arbitrary",), vmem),
        name="moe_combine",
    )(dest, x, gates, gf, ys)


def _moe_plan(ids, counts, n_groups, n_experts, n_rows, tb):
    cnt = counts[0, n_groups:n_groups + n_experts].astype(jnp.int32)
    ends = jnp.cumsum(cnt)
    starts = ends - cnt
    dest = (starts[ids[:, 0:TOP_K]] + ids[:, TOP_K:2 * TOP_K]).reshape(-1).astype(jnp.int32)

    first_blk = starts // tb
    n_items = jnp.where(cnt > 0, (ends - 1) // tb - first_blk + 1, 0)
    item_end = jnp.cumsum(n_items)
    item_start = item_end - n_items
    n_work = item_end[-1]
    max_work = n_rows // tb + n_experts
    w = jnp.minimum(jnp.arange(max_work, dtype=jnp.int32), n_work - 1)
    we = jnp.searchsorted(item_end, w, side="right").astype(jnp.int32)
    wb = (first_blk[we] + (w - item_start[we])).astype(jnp.int32)
    plan = (wb, we, starts.astype(jnp.int32), ends.astype(jnp.int32), n_work.reshape(1).astype(jnp.int32))
    return dest, plan


def _hierarchical_moe(x, g, rg_w, rg_b, re_w, re_b, w_gate, w_up, w_down, gf, final_norm):
    s, d = x.shape
    n_groups, _, epg = re_w.shape
    n_experts = n_groups * epg
    lanes = V7X_LANES
    assert n_groups + n_experts <= lanes
    wr = jnp.concatenate([rg_w, jnp.transpose(re_w, (1, 0, 2)).reshape(d, n_experts)], axis=1)
    wr = jnp.pad(wr, ((0, 0), (0, lanes - wr.shape[1])))
    br = jnp.pad(jnp.concatenate([rg_b, re_b.reshape(-1)]), (0, lanes - n_groups - n_experts)).reshape(1, lanes)

    ids, gates, counts = _router(x, g, wr, br, n_groups, epg)
    tb = _tile(TOP_K * s, 256)
    dest, plan = _moe_plan(ids, counts, n_groups, n_experts, TOP_K * s, tb)
    xs = _dispatch(dest, x, g)
    ys = _experts(plan, xs, w_gate, w_up, w_down, tb)
    return _combine(dest, x, gates, ys, gf, final_norm)


def kernel(x, mix_norm, ffn_norm, final_norm, conv_w_in, conv_w, conv_w_out, attn_w_in, attn_b_f, attn_w_out,
           router_group_w, router_group_b, router_expert_w, router_expert_b, w_gate, w_up, w_down):
    b, s, d = x.shape
    assert b == 1 and mix_norm.shape[0] == 2
    n_heads = attn_b_f.shape[-1]
    lanes = V7X_LANES
    gf = final_norm.reshape(1, d)
    h = x.reshape(s, d)

    go, z = _conv_inproj(h, mix_norm[0].reshape(1, d), conv_w_in[0])
    h = _conv_outproj(go, z, conv_w[0], conv_w_out[0], h)
    h = _hierarchical_moe(h, ffn_norm[0].reshape(1, d), router_group_w[0], router_group_b[0],
                          router_expert_w[0], router_expert_b[0], w_gate[0], w_up[0], w_down[0],
                          gf, final_norm=False)

    wf = jnp.pad(attn_w_in[0][:, 3 * d:], ((0, 0), (0, lanes - n_heads)))
    bf = jnp.pad(attn_b_f[0], (0, lanes - n_heads)).reshape(1, lanes)
    qkv, fl = _attn_inproj(h, mix_norm[1].reshape(1, d), attn_w_in[0], wf, bf, n_heads)
    c3 = _forget_cumsum(fl)[:n_heads].reshape(n_heads, 1, s)
    o = _fox_attention(qkv, c3, n_heads, d)
    h = _matmul_residual(o, attn_w_out[0], h)
    h = _hierarchical_moe(h, ffn_norm[1].reshape(1, d), router_group_w[1], router_group_b[1],
                          router_expert_w[1], router_expert_b[1], w_gate[1], w_up[1], w_down[1],
                          gf, final_norm=True)
    return h.reshape(b, s, d)
```

```python
import functools
import math

import jax
import jax.numpy as jnp
from jax import lax
from jax.experimental import pallas as pl
from jax.experimental.pallas import tpu as pltpu

RMS_EPS = 1e-6
NEG_INF = -1e30
TOP_K = 2
LOG2E = 1.4426950408889634
DMA_LOOP_UNROLL = 8

V7X_LANES = 128
V7X_BF16_SUBLANES = 16
V7X_VMEM_BYTES = 64 * 1024 * 1024

_BF16 = jnp.bfloat16
_F32 = jnp.float32


def _tile(n, pref):
    t = min(n, pref)
    while n % t:
        t //= 2
    return t


def _params(semantics, vmem_bytes):
    limit = min(int(vmem_bytes), V7X_VMEM_BYTES - 4 * 1024 * 1024)
    return pltpu.CompilerParams(dimension_semantics=semantics, vmem_limit_bytes=limit)


def _nbytes(shape, dtype):
    return math.prod(shape) * jnp.dtype(dtype).itemsize


def _rmsnorm(x, g):
    ms = jnp.mean(x * x, axis=-1, keepdims=True)
    return x * lax.rsqrt(ms + RMS_EPS) * g


def _mm(a, b):
    return jnp.dot(a, b, preferred_element_type=_F32)


def _conv_inproj_kernel(x_ref, g_ref, wgo_ref, wgi_ref, wu_ref, go_ref, z_ref, hn_ref):
    @pl.when(pl.program_id(1) == 0)
    def _():
        hn_ref[...] = _rmsnorm(x_ref[...], g_ref[...]).astype(_BF16)

    hn = hn_ref[...]
    go_ref[...] = _mm(hn, wgo_ref[...].astype(_BF16)).astype(go_ref.dtype)
    gate_in = _mm(hn, wgi_ref[...].astype(_BF16))
    u = _mm(hn, wu_ref[...].astype(_BF16))
    z_ref[...] = (gate_in * u).astype(z_ref.dtype)


def _conv_inproj(x, g, w_in):
    s, d = x.shape
    tm, tn = _tile(s, 1024), _tile(d, 256)
    nj = d // tn
    vmem = (2 * _nbytes((tm, d), _F32) + _nbytes((tm, d), _BF16) + 6 * _nbytes((d, tn), _F32)
            + 3 * _nbytes((d, tn), _BF16) + 4 * _nbytes((tm, tn), _BF16) + 4 * _nbytes((tm, tn), _F32)
            + 2 * _nbytes((tm, d), _F32))
    return pl.pallas_call(
        _conv_inproj_kernel,
        out_shape=(jax.ShapeDtypeStruct((s, d), _BF16), jax.ShapeDtypeStruct((s, d), _BF16)),
        grid=(s // tm, nj),
        in_specs=[
            pl.BlockSpec((tm, d), lambda i, j: (i, 0)),
            pl.BlockSpec((1, d), lambda i, j: (0, 0)),
            pl.BlockSpec((d, tn), lambda i, j: (0, j)),
            pl.BlockSpec((d, tn), lambda i, j: (0, j + nj)),
            pl.BlockSpec((d, tn), lambda i, j: (0, j + 2 * nj)),
        ],
        out_specs=(pl.BlockSpec((tm, tn), lambda i, j: (i, j)),
                   pl.BlockSpec((tm, tn), lambda i, j: (i, j))),
        scratch_shapes=[pltpu.VMEM((tm, d), _BF16)],
        compiler_params=_params(("parallel", "arbitrary"), vmem),
        name="conv_inproj",
    )(x, g, w_in, w_in, w_in)


def _conv_outproj_kernel(go_ref, z_ref, halo_ref, cw_ref, w_ref, x_ref, o_ref, y_ref, *, chunk):
    i = pl.program_id(0)

    @pl.when(pl.program_id(1) == 0)
    def _():
        tm, d = y_ref.shape
        row = lax.broadcasted_iota(jnp.int32, (tm, chunk), 0)
        keep = (i > 0).astype(_F32)
        last = halo_ref.shape[0] - 1
        for c in range(d // chunk):
            cs = slice(c * chunk, (c + 1) * chunk)
            z = z_ref[:, cs].astype(_F32)
            halo = halo_ref[:, cs].astype(_F32) * keep
            h1 = halo[last:last + 1, :]
            h2 = halo[last - 1:last, :]
            z1 = jnp.where(row == 0, h1, pltpu.roll(z, 1, axis=0))
            z2 = jnp.where(row == 0, h2, jnp.where(row == 1, h1, pltpu.roll(z, 2, axis=0)))
            cw = cw_ref[:, cs]
            conv = cw[0:1, :] * z2 + cw[1:2, :] * z1 + cw[2:3, :] * z
            y_ref[:, cs] = (go_ref[:, cs].astype(_F32) * conv).astype(_BF16)

    o_ref[...] = x_ref[...] + _mm(y_ref[...], w_ref[...].astype(_BF16))


def _conv_outproj(go, z, conv_w, w_out, x):
    s, d = x.shape
    tm, tn = _tile(s, 1024), _tile(d, 512)
    hb = V7X_BF16_SUBLANES
    chunk = _tile(d, 512)
    vmem = (4 * _nbytes((tm, d), _BF16) + _nbytes((tm, d), _BF16) + 2 * _nbytes((d, tn), _F32)
            + _nbytes((d, tn), _BF16) + 6 * _nbytes((tm, tn), _F32) + 8 * _nbytes((tm, chunk), _F32))
    return pl.pallas_call(
        functools.partial(_conv_outproj_kernel, chunk=chunk),
        out_shape=jax.ShapeDtypeStruct((s, d), _F32),
        grid=(s // tm, d // tn),
        in_specs=[
            pl.BlockSpec((tm, d), lambda i, j: (i, 0)),
            pl.BlockSpec((tm, d), lambda i, j: (i, 0)),
            pl.BlockSpec((hb, d), lambda i, j: (jnp.maximum(i * (tm // hb) - 1, 0), 0)),
            pl.BlockSpec((conv_w.shape[0], d), lambda i, j: (0, 0)),
            pl.BlockSpec((d, tn), lambda i, j: (0, j)),
            pl.BlockSpec((tm, tn), lambda i, j: (i, j)),
        ],
        out_specs=pl.BlockSpec((tm, tn), lambda i, j: (i, j)),
        scratch_shapes=[pltpu.VMEM((tm, d), _BF16)],
        compiler_params=_params(("parallel", "arbitrary"), vmem),
        name="conv_outproj",
    )(go, z, z, conv_w, w_out, x)


def _attn_inproj_kernel(x_ref, g_ref, w_ref, wf_ref, bf_ref, qkv_ref, fl_ref, hn_ref, *, n_q_blocks, q_scale):
    j = pl.program_id(1)

    @pl.when(j == 0)
    def _():
        hn = _rmsnorm(x_ref[...], g_ref[...]).astype(_BF16)
        hn_ref[...] = hn
        fl_ref[...] = _mm(hn, wf_ref[...].astype(_BF16)) + bf_ref[...]

    res = _mm(hn_ref[...], w_ref[...].astype(_BF16)) * jnp.where(j < n_q_blocks, q_scale, 1.0)
    res = res.astype(qkv_ref.dtype)
    hd = qkv_ref.shape[2]
    for h in range(qkv_ref.shape[0]):
        qkv_ref[h] = res[:, h * hd:(h + 1) * hd]


def _attn_inproj(x, g, w_in, wf, bf, n_heads):
    s, d = x.shape
    hd = d // n_heads
    tm, tn = _tile(s, 1024), _tile(d, 512)
    hpb = tn // hd
    kern = functools.partial(_attn_inproj_kernel, n_q_blocks=d // tn, q_scale=LOG2E / math.sqrt(hd))
    vmem = (2 * _nbytes((tm, d), _F32) + _nbytes((tm, d), _BF16) + 2 * _nbytes((d, tn), _F32)
            + _nbytes((d, tn), _BF16) + 3 * _nbytes((tm, tn), _F32) + 2 * _nbytes((tm, d), _F32)
            + 4 * _nbytes((d, V7X_LANES), _F32))
    return pl.pallas_call(
        kern,
        out_shape=(jax.ShapeDtypeStruct((3 * n_heads, s, hd), _BF16),
                   jax.ShapeDtypeStruct((s, V7X_LANES), _F32)),
        grid=(s // tm, 3 * d // tn),
        in_specs=[
            pl.BlockSpec((tm, d), lambda i, j: (i, 0)),
            pl.BlockSpec((1, d), lambda i, j: (0, 0)),
            pl.BlockSpec((d, tn), lambda i, j: (0, j)),
            pl.BlockSpec((d, V7X_LANES), lambda i, j: (0, 0)),
            pl.BlockSpec((1, V7X_LANES), lambda i, j: (0, 0)),
        ],
        out_specs=(pl.BlockSpec((hpb, tm, hd), lambda i, j: (j, i, 0)),
                   pl.BlockSpec((tm, V7X_LANES), lambda i, j: (i, 0))),
        scratch_shapes=[pltpu.VMEM((tm, d), _BF16)],
        compiler_params=_params(("parallel", "arbitrary"), vmem),
        name="attn_inproj",
    )(x, g, w_in, wf, bf)


def _forget_cumsum_kernel(fl_ref, kb_ref, carry_ref, *, n_heads):
    @pl.when(pl.program_id(0) == 0)
    def _():
        carry_ref[...] = jnp.zeros_like(carry_ref)

    fl = fl_ref[...]
    ls = jnp.minimum(fl, 0.0) - jnp.log1p(jnp.exp(-jnp.abs(fl)))
    tc = ls.shape[0]
    row = lax.broadcasted_iota(jnp.int32, ls.shape, 0)
    shift = 1
    while shift < tc:
        ls = jnp.where(row >= shift, ls + pltpu.roll(ls, shift, axis=0), ls)
        shift *= 2
    ls = ls + carry_ref[...]
    carry_ref[...] = ls[tc - 1:tc, :]

    lane = lax.broadcasted_iota(jnp.int32, ls.shape, 1)
    bias = jnp.where(lane < n_heads, ls * (-LOG2E), 0.0)
    hi = bias.astype(_BF16).astype(_F32)
    rem = bias - hi
    mid = rem.astype(_BF16).astype(_F32)
    lo = (rem - mid).astype(_BF16).astype(_F32)
    pieces = hi + pltpu.roll(mid, n_heads, axis=1) + pltpu.roll(lo, 2 * n_heads, axis=1)
    kb_ref[...] = pieces.astype(_BF16)


def _forget_cumsum(fl, n_heads):
    s, lanes = fl.shape
    assert 3 * n_heads <= lanes
    tc = _tile(s, 2048)
    return pl.pallas_call(
        functools.partial(_forget_cumsum_kernel, n_heads=n_heads),
        out_shape=jax.ShapeDtypeStruct((s, lanes), _BF16),
        grid=(s // tc,),
        in_specs=[pl.BlockSpec((tc, lanes), lambda i: (i, 0))],
        out_specs=pl.BlockSpec((tc, lanes), lambda i: (i, 0)),
        scratch_shapes=[pltpu.VMEM((1, lanes), _F32)],
        compiler_params=_params(("arbitrary",), 16 * _nbytes((tc, lanes), _F32)),
        name="forget_cumsum",
    )(fl)


def _fox_attn_kernel(q_ref, k_ref, v_ref, kb_ref, o_ref, m_ref, acc_ref, *, n_heads, tk):
    h = pl.program_id(0)
    i = pl.program_id(1)
    tq, hd = q_ref.shape
    per_q = tq // tk
    lane = lax.broadcasted_iota(jnp.int32, (tq, hd), 1)
    head_sel = jnp.where((lane == h) | (lane == n_heads + h) | (lane == 2 * n_heads + h), 1.0, 0.0)
    q_ext = jnp.concatenate([q_ref[...], head_sel.astype(_BF16)], axis=1)
    ones_col = jnp.where(lax.broadcasted_iota(jnp.int32, (tk, hd), 1) == 0, 1.0, 0.0).astype(_BF16)

    m_ref[...] = jnp.full_like(m_ref, -jnp.inf)
    acc_ref[...] = jnp.zeros_like(acc_ref)

    def block(j, diag, m_prev, acc):
        k0 = pl.multiple_of(j * tk, tk)
        k_ext = jnp.concatenate([k_ref[pl.ds(k0, tk), :], kb_ref[pl.ds(k0, tk), :]], axis=1)
        v_ext = jnp.concatenate([v_ref[pl.ds(k0, tk), :], ones_col], axis=1)
        s = lax.dot_general(q_ext, k_ext, (((1,), (1,)), ((), ())), preferred_element_type=_F32)
        if diag is not None:
            qi = lax.broadcasted_iota(jnp.int32, s.shape, 0)
            ki = lax.broadcasted_iota(jnp.int32, s.shape, 1) + diag * tk
            s = jnp.where(ki <= qi, s, NEG_INF)
        chunks = [s[:, c * hd:(c + 1) * hd] for c in range(tk // hd)]
        m_new = jnp.maximum(m_prev, jnp.max(functools.reduce(jnp.maximum, chunks), axis=-1, keepdims=True))
        alpha = jnp.exp2(m_prev - m_new)
        p = jnp.concatenate([jnp.exp2(c - m_new).astype(_BF16) for c in chunks], axis=1)
        return m_new, jnp.concatenate([alpha, alpha], axis=1) * acc + _mm(p, v_ext)

    def run(blocks):
        m, acc = m_ref[...], acc_ref[...]
        for j, diag in blocks:
            m, acc = block(j, diag, m, acc)
        m_ref[...] = m
        acc_ref[...] = acc

    def body(t, carry):
        run([(2 * per_q * t + u, None) for u in range(2 * per_q)])
        return carry

    lax.fori_loop(0, i // 2, body, 0)
    diagonal = [(per_q * i + u, u) for u in range(per_q)]

    @pl.when(i % 2 == 1)
    def _():
        run([(per_q * (i - 1) + u, None) for u in range(per_q)] + diagonal)

    @pl.when(i % 2 == 0)
    def _():
        run(diagonal)

    acc = acc_ref[...]
    o_ref[...] = (acc[:, :hd] / acc[:, hd:hd + 1]).astype(o_ref.dtype)


def _fox_attention(qkv, kb, n_heads, d):
    _, s, hd = qkv.shape
    assert kb.shape == (s, hd)
    tq = _tile(s, 1024)
    tk = _tile(tq, 512)
    vmem = (10 * _nbytes((s, hd), _BF16) + 12 * _nbytes((tq, 2 * hd), _F32) + 6 * (tq // tk) * _nbytes((tq, tk), _F32))
    return pl.pallas_call(
        functools.partial(_fox_attn_kernel, n_heads=n_heads, tk=tk),
        out_shape=jax.ShapeDtypeStruct((s, d), _BF16),
        grid=(n_heads, s // tq),
        in_specs=[
            pl.BlockSpec((None, tq, hd), lambda h, i: (h, i, 0)),
            pl.BlockSpec((None, s, hd), lambda h, i: (n_heads + h, 0, 0)),
            pl.BlockSpec((None, s, hd), lambda h, i: (2 * n_heads + h, 0, 0)),
            pl.BlockSpec((s, hd), lambda h, i: (0, 0)),
        ],
        out_specs=pl.BlockSpec((tq, hd), lambda h, i: (i, h)),
        scratch_shapes=[pltpu.VMEM((tq, hd), _F32), pltpu.VMEM((tq, 2 * hd), _F32)],
        compiler_params=_params(("parallel", "arbitrary"), vmem),
        name="fox_attention",
    )(qkv, qkv, qkv, kb)


def _matmul_residual_kernel(a_ref, w_ref, x_ref, o_ref):
    o_ref[...] = x_ref[...] + _mm(a_ref[...], w_ref[...].astype(_BF16))


def _matmul_residual(a, w, x):
    s, d = x.shape
    tm, tn = _tile(s, 1024), _tile(d, 512)
    vmem = (2 * _nbytes((tm, d), _BF16) + 2 * _nbytes((d, tn), _F32) + _nbytes((d, tn), _BF16)
            + 6 * _nbytes((tm, tn), _F32))
    return pl.pallas_call(
        _matmul_residual_kernel,
        out_shape=jax.ShapeDtypeStruct((s, d), _F32),
        grid=(s // tm, d // tn),
        in_specs=[
            pl.BlockSpec((tm, d), lambda i, j: (i, 0)),
            pl.BlockSpec((d, tn), lambda i, j: (0, j)),
            pl.BlockSpec((tm, tn), lambda i, j: (i, j)),
        ],
        out_specs=pl.BlockSpec((tm, tn), lambda i, j: (i, j)),
        compiler_params=_params(("parallel", "arbitrary"), vmem),
        name="attn_outproj",
    )(a, w, x)


def _router_kernel(x_ref, g_ref, wr_ref, br_ref, ids_ref, gates_ref, counts_ref, tril_ref, carry_ref,
                   *, n_groups, epg):
    i = pl.program_id(0)
    tm = x_ref.shape[0]
    lanes = wr_ref.shape[1]

    @pl.when(i == 0)
    def _():
        r = lax.broadcasted_iota(jnp.int32, (tm, tm), 0)
        c = lax.broadcasted_iota(jnp.int32, (tm, tm), 1)
        tril_ref[...] = (c < r).astype(_BF16)
        carry_ref[...] = jnp.zeros_like(carry_ref)

    hn = _rmsnorm(x_ref[...], g_ref[...])
    logits = _mm(hn.astype(_BF16), wr_ref[...].astype(_BF16)) + br_ref[...]
    lane = lax.broadcasted_iota(jnp.int32, (tm, lanes), 1)

    def first_argmax(vals):
        top = jnp.max(vals, axis=-1, keepdims=True)
        idx = jnp.min(jnp.where(vals == top, lane, lanes), axis=-1, keepdims=True)
        return top, idx

    gl = jnp.where(lane < n_groups, logits, -jnp.inf)
    g_top, g_sel = first_argmax(gl)
    p_g = 1.0 / jnp.sum(jnp.exp(gl - g_top), axis=-1, keepdims=True)
    lo = n_groups + g_sel * epg
    el = jnp.where((lane >= lo) & (lane < lo + epg), logits, -jnp.inf)
    v1, i1 = first_argmax(el)
    v2, i2 = first_argmax(jnp.where(lane == i1, -jnp.inf, el))
    t = jnp.exp(v2 - v1)
    den = 1.0 + t
    gate1 = (1.0 / den) * p_g
    gate2 = (t / den) * p_g

    sel1 = lane == i1
    sel2 = lane == i2
    onehot = (sel1 | sel2).astype(_F32)
    prefix = _mm(tril_ref[...], onehot.astype(_BF16)) + carry_ref[...]
    rank1 = jnp.sum(jnp.where(sel1, prefix, 0.0), axis=-1, keepdims=True).astype(jnp.int32)
    rank2 = jnp.sum(jnp.where(sel2, prefix, 0.0), axis=-1, keepdims=True).astype(jnp.int32)
    carry = carry_ref[...] + jnp.sum(onehot, axis=0, keepdims=True)
    carry_ref[...] = carry
    counts_ref[...] = carry

    ids = jnp.where(lane == 0, i1 - n_groups,
                    jnp.where(lane == 1, i2 - n_groups,
                              jnp.where(lane == 2, rank1, jnp.where(lane == 3, rank2, 0))))
    gates = jnp.where(lane == 0, gate1, jnp.where(lane == 1, gate2, 0.0))
    ids_ref[...] = ids[:, :ids_ref.shape[1]]
    gates_ref[...] = gates[:, :gates_ref.shape[1]]


def _router(x, g, wr, br, n_groups, epg):
    s, d = x.shape
    lanes = wr.shape[1]
    tm = _tile(s, 512)
    vmem = (4 * _nbytes((tm, d), _F32) + _nbytes((tm, tm), _BF16) + 4 * _nbytes((d, lanes), _F32)
            + 24 * _nbytes((tm, lanes), _F32) + 4 * _nbytes((tm, tm), jnp.int32))
    return pl.pallas_call(
        functools.partial(_router_kernel, n_groups=n_groups, epg=epg),
        out_shape=(jax.ShapeDtypeStruct((s, 8), jnp.int32), jax.ShapeDtypeStruct((s, 8), _F32),
                   jax.ShapeDtypeStruct((1, lanes), _F32)),
        grid=(s // tm,),
        in_specs=[
            pl.BlockSpec((tm, d), lambda i: (i, 0)),
            pl.BlockSpec((1, d), lambda i: (0, 0)),
            pl.BlockSpec((d, lanes), lambda i: (0, 0)),
            pl.BlockSpec((1, lanes), lambda i: (0, 0)),
        ],
        out_specs=(pl.BlockSpec((tm, 8), lambda i: (i, 0)), pl.BlockSpec((tm, 8), lambda i: (i, 0)),
                   pl.BlockSpec((1, lanes), lambda i: (0, 0))),
        scratch_shapes=[pltpu.VMEM((tm, tm), _BF16), pltpu.VMEM((1, lanes), _F32)],
        compiler_params=_params(("arbitrary",), vmem),
        name="moe_router",
    )(x, g, wr, br)


def _dispatch_kernel(dest_ref, x_ref, g_ref, xs_hbm, hn_ref, sem_ref):
    i = pl.program_id(0)
    n = pl.num_programs(0)
    tm = x_ref.shape[0]
    slot = i % 2

    def row_copy(s, r, dst):
        return pltpu.make_async_copy(hn_ref.at[s, pl.ds(r, 1), :], xs_hbm.at[pl.ds(dst, 1), :], sem_ref.at[s])

    def drain(s):
        def body(r, carry):
            row_copy(s, 0, 0).wait()
            row_copy(s, 0, 0).wait()
            return carry
        lax.fori_loop(0, tm, body, 0, unroll=DMA_LOOP_UNROLL)

    hn_ref[slot] = _rmsnorm(x_ref[...], g_ref[...])

    def issue(r, carry):
        a = TOP_K * (i * tm + r)
        row_copy(slot, r, dest_ref[a]).start(priority=0)
        row_copy(slot, r, dest_ref[a + 1]).start(priority=1)
        return carry

    lax.fori_loop(0, tm, issue, 0, unroll=DMA_LOOP_UNROLL)

    @pl.when(i > 0)
    def _():
        drain(1 - slot)

    @pl.when(i == n - 1)
    def _():
        drain(slot)


def _dispatch(dest, x, g):
    s, d = x.shape
    tm = _tile(s, 256)
    vmem = 6 * _nbytes((tm, d), _F32)
    return pl.pallas_call(
        _dispatch_kernel,
        out_shape=jax.ShapeDtypeStruct((TOP_K * s, d), _F32),
        grid_spec=pltpu.PrefetchScalarGridSpec(
            num_scalar_prefetch=1,
            grid=(s // tm,),
            in_specs=[pl.BlockSpec((tm, d), lambda i, dest: (i, 0)),
                      pl.BlockSpec((1, d), lambda i, dest: (0, 0))],
            out_specs=pl.BlockSpec(memory_space=pl.ANY),
            scratch_shapes=[pltpu.VMEM((2, tm, d), _F32), pltpu.SemaphoreType.DMA((2,))],
        ),
        compiler_params=_params(("arbitrary",), vmem),
        name="moe_dispatch",
    )(dest, x, g)


def _experts_kernel(wb_ref, we_ref, lo_ref, hi_ref, nw_ref, x_ref, wg_ref, wu_ref, wd_ref, o_ref,
                    wg_s, wu_s, wd_s, *, cast_rows):
    w = pl.program_id(0)

    @pl.when(w < nw_ref[0])
    def _():
        e = we_ref[w]
        b = wb_ref[w]
        prev = jnp.maximum(w - 1, 0)
        new_expert = (w == 0) | (we_ref[prev] != e)
        new_block = (w == 0) | (wb_ref[prev] != b)

        @pl.when(new_expert)
        def _():
            for src, dst in ((wg_ref, wg_s), (wu_ref, wu_s), (wd_ref, wd_s)):
                for r in range(0, src.shape[0], cast_rows):
                    dst[r:r + cast_rows, :] = src[r:r + cast_rows, :].astype(_BF16)

        tb = x_ref.shape[0]
        x = x_ref[...].astype(_BF16)
        gate = _mm(x, wg_s[...])
        up = _mm(x, wu_s[...])
        hid = gate * (1.0 / (1.0 + jnp.exp(-gate))) * up
        row = b * tb + lax.broadcasted_iota(jnp.int32, (tb, 1), 0)
        hid = jnp.where((row >= lo_ref[e]) & (row < hi_ref[e]), hid, 0.0).astype(_BF16)
        out = _mm(hid, wd_s[...])

        @pl.when(new_block)
        def _():
            o_ref[...] = out

        @pl.when(jnp.logical_not(new_block))
        def _():
            o_ref[...] += out


def _experts(plan, xs, w_gate, w_up, w_down, layer, tb):
    a, d = xs.shape
    f = w_gate.shape[3]
    wb, we, lo, hi, nw = plan
    cast_rows = _tile(f, 256)
    vmem = (4 * _nbytes((tb, d), _F32) + 6 * _nbytes((d, f), _F32) + 3 * _nbytes((d, f), _BF16)
            + 6 * _nbytes((tb, f), _F32) + 3 * _nbytes((tb, d), _F32) + 2 * _nbytes((cast_rows, d), _F32))
    return pl.pallas_call(
        functools.partial(_experts_kernel, cast_rows=cast_rows),
        out_shape=jax.ShapeDtypeStruct((a, d), _F32),
        grid_spec=pltpu.PrefetchScalarGridSpec(
            num_scalar_prefetch=5,
            grid=(wb.shape[0],),
            in_specs=[
                pl.BlockSpec((tb, d), lambda w, wb, we, lo, hi, nw: (wb[w], 0)),
                pl.BlockSpec((None, None, d, f), lambda w, wb, we, lo, hi, nw: (layer, we[w], 0, 0)),
                pl.BlockSpec((None, None, d, f), lambda w, wb, we, lo, hi, nw: (layer, we[w], 0, 0)),
                pl.BlockSpec((None, None, f, d), lambda w, wb, we, lo, hi, nw: (layer, we[w], 0, 0)),
            ],
            out_specs=pl.BlockSpec((tb, d), lambda w, wb, we, lo, hi, nw: (wb[w], 0)),
            scratch_shapes=[pltpu.VMEM((d, f), _BF16), pltpu.VMEM((d, f), _BF16), pltpu.VMEM((f, d), _BF16)],
        ),
        compiler_params=_params(("arbitrary",), vmem),
        name="moe_experts",
    )(wb, we, lo, hi, nw, xs, w_gate, w_up, w_down)


def _combine_kernel(dest_ref, x_ref, gates_ref, gf_ref, ys_hbm, o_ref, buf_ref, sem_ref, *, final_norm):
    i = pl.program_id(0)
    n = pl.num_programs(0)
    tm = x_ref.shape[0]
    slot = i % 2

    def row_copy(s, k, r, src):
        return pltpu.make_async_copy(ys_hbm.at[pl.ds(src, 1), :], buf_ref.at[s, k, pl.ds(r, 1), :],
                                     sem_ref.at[s])

    def issue(tile, s):
        def body(r, carry):
            a = TOP_K * (tile * tm + r)
            row_copy(s, 0, r, dest_ref[a]).start(priority=0)
            row_copy(s, 1, r, dest_ref[a + 1]).start(priority=1)
            return carry
        lax.fori_loop(0, tm, body, 0, unroll=DMA_LOOP_UNROLL)

    @pl.when(i == 0)
    def _():
        issue(0, 0)

    @pl.when(i + 1 < n)
    def _():
        issue(i + 1, 1 - slot)

    def drain(r, carry):
        row_copy(slot, 0, 0, 0).wait()
        row_copy(slot, 1, 0, 0).wait()
        return carry

    lax.fori_loop(0, tm, drain, 0, unroll=DMA_LOOP_UNROLL)

    gates = gates_ref[...]
    y = gates[:, 0:1] * buf_ref[slot, 0] + gates[:, 1:2] * buf_ref[slot, 1]
    out = x_ref[...] + y
    if final_norm:
        out = _rmsnorm(out, gf_ref[...])
    o_ref[...] = out


def _combine(dest, x, gates, ys, gf, final_norm):
    s, d = x.shape
    tm = _tile(s, 256)
    vmem = 4 * _nbytes((tm, d), _F32) + 4 * _nbytes((tm, d), _F32) + 6 * _nbytes((tm, d), _F32)
    return pl.pallas_call(
        functools.partial(_combine_kernel, final_norm=final_norm),
        out_shape=jax.ShapeDtypeStruct((s, d), _F32),
        grid_spec=pltpu.PrefetchScalarGridSpec(
            num_scalar_prefetch=1,
            grid=(s // tm,),
            in_specs=[
                pl.BlockSpec((tm, d), lambda i, dest: (i, 0)),
                pl.BlockSpec((tm, gates.shape[1]), lambda i, dest: (i, 0)),
                pl.BlockSpec((1, d), lambda i, dest: (0, 0)),
                pl.BlockSpec(memory_space=pl.ANY),
            ],
            out_specs=pl.BlockSpec((tm, d), lambda i, dest: (i, 0)),
            scratch_shapes=[pltpu.VMEM((2, TOP_K, tm, d), _F32), pltpu.SemaphoreType.DMA((2,))],
        ),
        compiler_params=_params(("arbitrary",), vmem),
        name="moe_combine",
    )(dest, x, gates, gf, ys)


def _moe_plan(ids, counts, n_groups, n_experts, n_rows, tb):
    cnt = counts[0, n_groups:n_groups + n_experts].astype(jnp.int32)
    ends = jnp.cumsum(cnt)
    starts = ends - cnt
    experts = jnp.arange(n_experts, dtype=jnp.int32)

    def lookup(table, idx):
        return jnp.sum(jnp.where(idx[..., None] == experts, table, 0), axis=-1)

    dest = (lookup(starts, ids[:, 0:TOP_K]) + ids[:, TOP_K:2 * TOP_K]).reshape(-1).astype(jnp.int32)

    first_blk = starts // tb
    n_items = jnp.where(cnt > 0, (ends - 1) // tb - first_blk + 1, 0)
    item_end = jnp.cumsum(n_items)
    n_work = item_end[-1]
    max_work = n_rows // tb + n_experts
    w = jnp.minimum(jnp.arange(max_work, dtype=jnp.int32), n_work - 1)
    we = jnp.sum((item_end[None, :] <= w[:, None]).astype(jnp.int32), axis=1)
    wb = (lookup(first_blk - (item_end - n_items), we) + w).astype(jnp.int32)
    plan = (wb, we, starts.astype(jnp.int32), ends.astype(jnp.int32), n_work.reshape(1).astype(jnp.int32))
    return dest, plan


def _hierarchical_moe(x, g, rg_w, rg_b, re_w, re_b, w_gate, w_up, w_down, layer, gf, final_norm):
    s, d = x.shape
    n_groups, _, epg = re_w.shape
    n_experts = n_groups * epg
    lanes = V7X_LANES
    assert n_groups + n_experts <= lanes
    wr = jnp.concatenate([rg_w, jnp.transpose(re_w, (1, 0, 2)).reshape(d, n_experts)], axis=1)
    wr = jnp.pad(wr, ((0, 0), (0, lanes - wr.shape[1])))
    br = jnp.pad(jnp.concatenate([rg_b, re_b.reshape(-1)]), (0, lanes - n_groups - n_experts)).reshape(1, lanes)

    ids, gates, counts = _router(x, g, wr, br, n_groups, epg)
    tb = _tile(TOP_K * s, 256)
    dest, plan = _moe_plan(ids, counts, n_groups, n_experts, TOP_K * s, tb)
    xs = _dispatch(dest, x, g)
    ys = _experts(plan, xs, w_gate, w_up, w_down, layer, tb)
    return _combine(dest, x, gates, ys, gf, final_norm)


def kernel(x, mix_norm, ffn_norm, final_norm, conv_w_in, conv_w, conv_w_out, attn_w_in, attn_b_f, attn_w_out,
           router_group_w, router_group_b, router_expert_w, router_expert_b, w_gate, w_up, w_down):
    b, s, d = x.shape
    assert b == 1 and mix_norm.shape[0] == 2
    n_heads = attn_b_f.shape[-1]
    lanes = V7X_LANES
    gf = final_norm.reshape(1, d)
    h = x.reshape(s, d)

    go, z = _conv_inproj(h, mix_norm[0].reshape(1, d), conv_w_in[0])
    h = _conv_outproj(go, z, conv_w[0], conv_w_out[0], h)
    h = _hierarchical_moe(h, ffn_norm[0].reshape(1, d), router_group_w[0], router_group_b[0],
                          router_expert_w[0], router_expert_b[0], w_gate, w_up, w_down, 0,
                          gf, final_norm=False)

    wf = jnp.pad(attn_w_in[0][:, 3 * d:], ((0, 0), (0, lanes - n_heads)))
    bf = jnp.pad(attn_b_f[0], (0, lanes - n_heads)).reshape(1, lanes)
    qkv, fl = _attn_inproj(h, mix_norm[1].reshape(1, d), attn_w_in[0], wf, bf, n_heads)
    kb = _forget_cumsum(fl, n_heads)
    o = _fox_attention(qkv, kb, n_heads, d)
    h = _matmul_residual(o, attn_w_out[0], h)
    h = _hierarchical_moe(h, ffn_norm[1].reshape(1, d), router_group_w[1], router_group_b[1],
                          router_expert_w[1], router_expert_b[1], w_gate, w_up, w_down, 1,
                          gf, final_norm=True)
    return h.reshape(b, s, d)
```

```python
import functools
import math

import jax
import jax.numpy as jnp
from jax import lax
from jax.experimental import pallas as pl
from jax.experimental.pallas import tpu as pltpu

RMS_EPS = 1e-6
NEG_INF = -1e30
TOP_K = 2
LOG2E = 1.4426950408889634
DMA_LOOP_UNROLL = 8

V7X_LANES = 128
V7X_BF16_SUBLANES = 16
V7X_VMEM_BYTES = 64 * 1024 * 1024

_BF16 = jnp.bfloat16
_F32 = jnp.float32


def _tile(n, pref):
    t = min(n, pref)
    while n % t:
        t //= 2
    return t


def _params(semantics, vmem_bytes):
    limit = min(int(vmem_bytes), V7X_VMEM_BYTES - 4 * 1024 * 1024)
    return pltpu.CompilerParams(dimension_semantics=semantics, vmem_limit_bytes=limit)


def _nbytes(shape, dtype):
    return math.prod(shape) * jnp.dtype(dtype).itemsize


def _rmsnorm(x, g):
    ms = jnp.mean(x * x, axis=-1, keepdims=True)
    return x * lax.rsqrt(ms + RMS_EPS) * g


def _mm(a, b):
    return jnp.dot(a, b, preferred_element_type=_F32)


def _conv_inproj_kernel(x_ref, g_ref, wgo_ref, wgi_ref, wu_ref, go_ref, z_ref, hn_ref):
    @pl.when(pl.program_id(1) == 0)
    def _():
        hn_ref[...] = _rmsnorm(x_ref[...], g_ref[...]).astype(_BF16)

    hn = hn_ref[...]
    go_ref[...] = _mm(hn, wgo_ref[...].astype(_BF16)).astype(go_ref.dtype)
    gate_in = _mm(hn, wgi_ref[...].astype(_BF16))
    u = _mm(hn, wu_ref[...].astype(_BF16))
    z_ref[...] = (gate_in * u).astype(z_ref.dtype)


def _conv_inproj(x, g, w_in):
    s, d = x.shape
    tm, tn = _tile(s, 1024), _tile(d, 256)
    nj = d // tn
    vmem = (2 * _nbytes((tm, d), _F32) + _nbytes((tm, d), _BF16) + 6 * _nbytes((d, tn), _F32)
            + 3 * _nbytes((d, tn), _BF16) + 4 * _nbytes((tm, tn), _BF16) + 4 * _nbytes((tm, tn), _F32)
            + 2 * _nbytes((tm, d), _F32))
    return pl.pallas_call(
        _conv_inproj_kernel,
        out_shape=(jax.ShapeDtypeStruct((s, d), _BF16), jax.ShapeDtypeStruct((s, d), _BF16)),
        grid=(s // tm, nj),
        in_specs=[
            pl.BlockSpec((tm, d), lambda i, j: (i, 0)),
            pl.BlockSpec((1, d), lambda i, j: (0, 0)),
            pl.BlockSpec((d, tn), lambda i, j: (0, j)),
            pl.BlockSpec((d, tn), lambda i, j: (0, j + nj)),
            pl.BlockSpec((d, tn), lambda i, j: (0, j + 2 * nj)),
        ],
        out_specs=(pl.BlockSpec((tm, tn), lambda i, j: (i, j)),
                   pl.BlockSpec((tm, tn), lambda i, j: (i, j))),
        scratch_shapes=[pltpu.VMEM((tm, d), _BF16)],
        compiler_params=_params(("parallel", "arbitrary"), vmem),
        name="conv_inproj",
    )(x, g, w_in, w_in, w_in)


def _conv_outproj_kernel(go_ref, z_ref, halo_ref, cw_ref, w_ref, x_ref, o_ref, y_ref, *, chunk):
    i = pl.program_id(0)

    @pl.when(pl.program_id(1) == 0)
    def _():
        tm, d = y_ref.shape
        row = lax.broadcasted_iota(jnp.int32, (tm, chunk), 0)
        keep = (i > 0).astype(_F32)
        last = halo_ref.shape[0] - 1
        for c in range(d // chunk):
            cs = slice(c * chunk, (c + 1) * chunk)
            z = z_ref[:, cs].astype(_F32)
            halo = halo_ref[:, cs].astype(_F32) * keep
            h1 = halo[last:last + 1, :]
            h2 = halo[last - 1:last, :]
            z1 = jnp.where(row == 0, h1, pltpu.roll(z, 1, axis=0))
            z2 = jnp.where(row == 0, h2, jnp.where(row == 1, h1, pltpu.roll(z, 2, axis=0)))
            cw = cw_ref[:, cs]
            conv = cw[0:1, :] * z2 + cw[1:2, :] * z1 + cw[2:3, :] * z
            y_ref[:, cs] = (go_ref[:, cs].astype(_F32) * conv).astype(_BF16)

    o_ref[...] = x_ref[...] + _mm(y_ref[...], w_ref[...].astype(_BF16))


def _conv_outproj(go, z, conv_w, w_out, x):
    s, d = x.shape
    tm, tn = _tile(s, 1024), _tile(d, 512)
    hb = V7X_BF16_SUBLANES
    chunk = _tile(d, 512)
    vmem = (4 * _nbytes((tm, d), _BF16) + _nbytes((tm, d), _BF16) + 2 * _nbytes((d, tn), _F32)
            + _nbytes((d, tn), _BF16) + 6 * _nbytes((tm, tn), _F32) + 8 * _nbytes((tm, chunk), _F32))
    return pl.pallas_call(
        functools.partial(_conv_outproj_kernel, chunk=chunk),
        out_shape=jax.ShapeDtypeStruct((s, d), _F32),
        grid=(s // tm, d // tn),
        in_specs=[
            pl.BlockSpec((tm, d), lambda i, j: (i, 0)),
            pl.BlockSpec((tm, d), lambda i, j: (i, 0)),
            pl.BlockSpec((hb, d), lambda i, j: (jnp.maximum(i * (tm // hb) - 1, 0), 0)),
            pl.BlockSpec((conv_w.shape[0], d), lambda i, j: (0, 0)),
            pl.BlockSpec((d, tn), lambda i, j: (0, j)),
            pl.BlockSpec((tm, tn), lambda i, j: (i, j)),
        ],
        out_specs=pl.BlockSpec((tm, tn), lambda i, j: (i, j)),
        scratch_shapes=[pltpu.VMEM((tm, d), _BF16)],
        compiler_params=_params(("parallel", "arbitrary"), vmem),
        name="conv_outproj",
    )(go, z, z, conv_w, w_out, x)


def _attn_inproj_kernel(x_ref, g_ref, w_ref, wf_ref, bf_ref, qkv_ref, fl_ref, hn_ref, *, n_q_blocks, q_scale):
    j = pl.program_id(1)

    @pl.when(j == 0)
    def _():
        hn = _rmsnorm(x_ref[...], g_ref[...]).astype(_BF16)
        hn_ref[...] = hn
        fl_ref[...] = _mm(hn, wf_ref[...].astype(_BF16)) + bf_ref[...]

    res = _mm(hn_ref[...], w_ref[...].astype(_BF16)) * jnp.where(j < n_q_blocks, q_scale, 1.0)
    res = res.astype(qkv_ref.dtype)
    hd = qkv_ref.shape[2]
    for h in range(qkv_ref.shape[0]):
        qkv_ref[h] = res[:, h * hd:(h + 1) * hd]


def _attn_inproj(x, g, w_in, wf, bf, n_heads):
    s, d = x.shape
    hd = d // n_heads
    tm, tn = _tile(s, 1024), _tile(d, 512)
    hpb = tn // hd
    kern = functools.partial(_attn_inproj_kernel, n_q_blocks=d // tn, q_scale=LOG2E / math.sqrt(hd))
    vmem = (2 * _nbytes((tm, d), _F32) + _nbytes((tm, d), _BF16) + 2 * _nbytes((d, tn), _F32)
            + _nbytes((d, tn), _BF16) + 3 * _nbytes((tm, tn), _F32) + 2 * _nbytes((tm, d), _F32)
            + 4 * _nbytes((d, V7X_LANES), _F32))
    return pl.pallas_call(
        kern,
        out_shape=(jax.ShapeDtypeStruct((3 * n_heads, s, hd), _BF16),
                   jax.ShapeDtypeStruct((s, V7X_LANES), _F32)),
        grid=(s // tm, 3 * d // tn),
        in_specs=[
            pl.BlockSpec((tm, d), lambda i, j: (i, 0)),
            pl.BlockSpec((1, d), lambda i, j: (0, 0)),
            pl.BlockSpec((d, tn), lambda i, j: (0, j)),
            pl.BlockSpec((d, V7X_LANES), lambda i, j: (0, 0)),
            pl.BlockSpec((1, V7X_LANES), lambda i, j: (0, 0)),
        ],
        out_specs=(pl.BlockSpec((hpb, tm, hd), lambda i, j: (j, i, 0)),
                   pl.BlockSpec((tm, V7X_LANES), lambda i, j: (i, 0))),
        scratch_shapes=[pltpu.VMEM((tm, d), _BF16)],
        compiler_params=_params(("parallel", "arbitrary"), vmem),
        name="attn_inproj",
    )(x, g, w_in, wf, bf)


def _forget_cumsum_kernel(fl_ref, kb_ref, carry_ref, *, n_heads):
    @pl.when(pl.program_id(0) == 0)
    def _():
        carry_ref[...] = jnp.zeros_like(carry_ref)

    fl = fl_ref[...]
    ls = jnp.minimum(fl, 0.0) - jnp.log1p(jnp.exp(-jnp.abs(fl)))
    tc = ls.shape[0]
    row = lax.broadcasted_iota(jnp.int32, ls.shape, 0)
    shift = 1
    while shift < tc:
        ls = jnp.where(row >= shift, ls + pltpu.roll(ls, shift, axis=0), ls)
        shift *= 2
    ls = ls + carry_ref[...]
    carry_ref[...] = ls[tc - 1:tc, :]

    lane = lax.broadcasted_iota(jnp.int32, ls.shape, 1)
    bias = jnp.where(lane < n_heads, ls * (-LOG2E), 0.0)
    hi = bias.astype(_BF16).astype(_F32)
    rem = bias - hi
    mid = rem.astype(_BF16).astype(_F32)
    lo = (rem - mid).astype(_BF16).astype(_F32)
    pieces = hi + pltpu.roll(mid, n_heads, axis=1) + pltpu.roll(lo, 2 * n_heads, axis=1)
    kb_ref[...] = pieces.astype(_BF16)


def _forget_cumsum(fl, n_heads):
    s, lanes = fl.shape
    assert 3 * n_heads <= lanes
    tc = _tile(s, 2048)
    return pl.pallas_call(
        functools.partial(_forget_cumsum_kernel, n_heads=n_heads),
        out_shape=jax.ShapeDtypeStruct((s, lanes), _BF16),
        grid=(s // tc,),
        in_specs=[pl.BlockSpec((tc, lanes), lambda i: (i, 0))],
        out_specs=pl.BlockSpec((tc, lanes), lambda i: (i, 0)),
        scratch_shapes=[pltpu.VMEM((1, lanes), _F32)],
        compiler_params=_params(("arbitrary",), 16 * _nbytes((tc, lanes), _F32)),
        name="forget_cumsum",
    )(fl)


def _fox_attn_kernel(q_ref, k_ref, v_ref, kb_ref, o_ref, m_ref, acc_ref, s_ref, p_ref, *, n_heads, tk):
    h = pl.program_id(0)
    i = pl.program_id(1)
    tq, hd = q_ref.shape
    per_q = tq // tk
    lane = lax.broadcasted_iota(jnp.int32, (tq, hd), 1)
    head_sel = jnp.where((lane == h) | (lane == n_heads + h) | (lane == 2 * n_heads + h), 1.0, 0.0)
    q_ext = jnp.concatenate([q_ref[...], head_sel.astype(_BF16)], axis=1)
    ones_col = jnp.where(lax.broadcasted_iota(jnp.int32, (tk, hd), 1) == 0, 1.0, 0.0).astype(_BF16)

    def scores(j):
        k0 = pl.multiple_of(j * tk, tk)
        k_ext = jnp.concatenate([k_ref[pl.ds(k0, tk), :], kb_ref[pl.ds(k0, tk), :]], axis=1)
        return lax.dot_general(q_ext, k_ext, (((1,), (1,)), ((), ())), preferred_element_type=_F32)

    def weighted_values(p, j):
        k0 = pl.multiple_of(j * tk, tk)
        return _mm(p, jnp.concatenate([v_ref[pl.ds(k0, tk), :], ones_col], axis=1))

    def probs(diag, s, m_prev):
        if diag is not None:
            qi = lax.broadcasted_iota(jnp.int32, s.shape, 0)
            ki = lax.broadcasted_iota(jnp.int32, s.shape, 1) + diag * tk
            s = jnp.where(ki <= qi, s, NEG_INF)
        chunks = [s[:, c * hd:(c + 1) * hd] for c in range(tk // hd)]
        m_new = jnp.maximum(m_prev, jnp.max(functools.reduce(jnp.maximum, chunks), axis=-1, keepdims=True))
        alpha = jnp.exp2(m_prev - m_new)
        p = jnp.concatenate([jnp.exp2(c - m_new).astype(_BF16) for c in chunks], axis=1)
        return m_new, jnp.concatenate([alpha, alpha], axis=1), p

    def run(blocks, prefetch, pending, defer):
        m, acc = m_ref[...], acc_ref[...]
        if pending:
            acc = acc + weighted_values(p_ref[...], blocks[0][0] - 1)
        for n, (j, diag) in enumerate(blocks):
            s = s_ref[...] if n == 0 else scores(j)
            m, alpha, p = probs(diag, s, m)
            if defer and n == len(blocks) - 1:
                acc = alpha * acc
                p_ref[...] = p
            else:
                acc = alpha * acc + weighted_values(p, j)
        if prefetch is not None:
            s_ref[...] = scores(prefetch)
        m_ref[...] = m
        acc_ref[...] = acc

    m_ref[...] = jnp.full_like(m_ref, -jnp.inf)
    acc_ref[...] = jnp.zeros_like(acc_ref)
    s_ref[...] = scores(0)
    group = 2 * per_q
    n_main = i // 2

    def main_group(t):
        return [(group * t + u, None) for u in range(group)]

    @pl.when(n_main > 0)
    def _():
        run(main_group(0), group, pending=False, defer=True)

    def body(t, carry):
        run(main_group(t), group * (t + 1), pending=True, defer=True)
        return carry

    lax.fori_loop(1, n_main, body, 0)
    diagonal = [(per_q * i + u, u) for u in range(per_q)]
    tail_odd = [(per_q * (i - 1) + u, None) for u in range(per_q)] + diagonal
    for odd, blocks in ((1, tail_odd), (0, diagonal)):
        for pending in (False, True):
            @pl.when((i % 2 == odd) & ((n_main > 0) == pending))
            def _(blocks=blocks, pending=pending):
                run(blocks, None, pending=pending, defer=False)

    acc = acc_ref[...]
    o_ref[...] = (acc[:, :hd] / acc[:, hd:hd + 1]).astype(o_ref.dtype)


def _fox_attention(qkv, kb, n_heads, d):
    _, s, hd = qkv.shape
    assert kb.shape == (s, hd)
    tq = _tile(s, 1024)
    tk = _tile(tq, 512)
    vmem = (10 * _nbytes((s, hd), _BF16) + 12 * _nbytes((tq, 2 * hd), _F32) + 6 * (tq // tk) * _nbytes((tq, tk), _F32))
    return pl.pallas_call(
        functools.partial(_fox_attn_kernel, n_heads=n_heads, tk=tk),
        out_shape=jax.ShapeDtypeStruct((s, d), _BF16),
        grid=(n_heads, s // tq),
        in_specs=[
            pl.BlockSpec((None, tq, hd), lambda h, i: (h, i, 0)),
            pl.BlockSpec((None, s, hd), lambda h, i: (n_heads + h, 0, 0)),
            pl.BlockSpec((None, s, hd), lambda h, i: (2 * n_heads + h, 0, 0)),
            pl.BlockSpec((s, hd), lambda h, i: (0, 0)),
        ],
        out_specs=pl.BlockSpec((tq, hd), lambda h, i: (i, h)),
        scratch_shapes=[pltpu.VMEM((tq, hd), _F32), pltpu.VMEM((tq, 2 * hd), _F32), pltpu.VMEM((tq, tk), _F32),
                        pltpu.VMEM((tq, tk), _BF16)],
        compiler_params=_params(("parallel", "arbitrary"), vmem),
        name="fox_attention",
    )(qkv, qkv, qkv, kb)


def _matmul_residual_kernel(a_ref, w_ref, x_ref, o_ref):
    o_ref[...] = x_ref[...] + _mm(a_ref[...], w_ref[...].astype(_BF16))


def _matmul_residual(a, w, x):
    s, d = x.shape
    tm, tn = _tile(s, 1024), _tile(d, 512)
    vmem = (2 * _nbytes((tm, d), _BF16) + 2 * _nbytes((d, tn), _F32) + _nbytes((d, tn), _BF16)
            + 6 * _nbytes((tm, tn), _F32))
    return pl.pallas_call(
        _matmul_residual_kernel,
        out_shape=jax.ShapeDtypeStruct((s, d), _F32),
        grid=(s // tm, d // tn),
        in_specs=[
            pl.BlockSpec((tm, d), lambda i, j: (i, 0)),
            pl.BlockSpec((d, tn), lambda i, j: (0, j)),
            pl.BlockSpec((tm, tn), lambda i, j: (i, j)),
        ],
        out_specs=pl.BlockSpec((tm, tn), lambda i, j: (i, j)),
        compiler_params=_params(("parallel", "arbitrary"), vmem),
        name="attn_outproj",
    )(a, w, x)


def _router_kernel(x_ref, g_ref, wr_ref, br_ref, ids_ref, gates_ref, counts_ref, tril_ref, carry_ref,
                   *, n_groups, epg):
    i = pl.program_id(0)
    tm = x_ref.shape[0]
    lanes = wr_ref.shape[1]

    @pl.when(i == 0)
    def _():
        r = lax.broadcasted_iota(jnp.int32, (tm, tm), 0)
        c = lax.broadcasted_iota(jnp.int32, (tm, tm), 1)
        tril_ref[...] = (c < r).astype(_BF16)
        carry_ref[...] = jnp.zeros_like(carry_ref)

    hn = _rmsnorm(x_ref[...], g_ref[...])
    logits = _mm(hn.astype(_BF16), wr_ref[...].astype(_BF16)) + br_ref[...]
    lane = lax.broadcasted_iota(jnp.int32, (tm, lanes), 1)

    def first_argmax(vals):
        top = jnp.max(vals, axis=-1, keepdims=True)
        idx = jnp.min(jnp.where(vals == top, lane, lanes), axis=-1, keepdims=True)
        return top, idx

    gl = jnp.where(lane < n_groups, logits, -jnp.inf)
    g_top, g_sel = first_argmax(gl)
    p_g = 1.0 / jnp.sum(jnp.exp(gl - g_top), axis=-1, keepdims=True)
    lo = n_groups + g_sel * epg
    el = jnp.where((lane >= lo) & (lane < lo + epg), logits, -jnp.inf)
    v1, i1 = first_argmax(el)
    v2, i2 = first_argmax(jnp.where(lane == i1, -jnp.inf, el))
    t = jnp.exp(v2 - v1)
    den = 1.0 + t
    gate1 = (1.0 / den) * p_g
    gate2 = (t / den) * p_g

    sel1 = lane == i1
    sel2 = lane == i2
    onehot = (sel1 | sel2).astype(_F32)
    prefix = _mm(tril_ref[...], onehot.astype(_BF16)) + carry_ref[...]
    rank1 = jnp.sum(jnp.where(sel1, prefix, 0.0), axis=-1, keepdims=True).astype(jnp.int32)
    rank2 = jnp.sum(jnp.where(sel2, prefix, 0.0), axis=-1, keepdims=True).astype(jnp.int32)
    carry = carry_ref[...] + jnp.sum(onehot, axis=0, keepdims=True)
    carry_ref[...] = carry
    counts_ref[...] = carry

    ids = jnp.where(lane == 0, i1 - n_groups,
                    jnp.where(lane == 1, i2 - n_groups,
                              jnp.where(lane == 2, rank1, jnp.where(lane == 3, rank2, 0))))
    gates = jnp.where(lane == 0, gate1, jnp.where(lane == 1, gate2, 0.0))
    ids_ref[...] = ids[:, :ids_ref.shape[1]]
    gates_ref[...] = gates[:, :gates_ref.shape[1]]


def _router(x, g, wr, br, n_groups, epg):
    s, d = x.shape
    lanes = wr.shape[1]
    tm = _tile(s, 512)
    vmem = (4 * _nbytes((tm, d), _F32) + _nbytes((tm, tm), _BF16) + 4 * _nbytes((d, lanes), _F32)
            + 24 * _nbytes((tm, lanes), _F32) + 4 * _nbytes((tm, tm), jnp.int32))
    return pl.pallas_call(
        functools.partial(_router_kernel, n_groups=n_groups, epg=epg),
        out_shape=(jax.ShapeDtypeStruct((s, 8), jnp.int32), jax.ShapeDtypeStruct((s, 8), _F32),
                   jax.ShapeDtypeStruct((1, lanes), _F32)),
        grid=(s // tm,),
        in_specs=[
            pl.BlockSpec((tm, d), lambda i: (i, 0)),
            pl.BlockSpec((1, d), lambda i: (0, 0)),
            pl.BlockSpec((d, lanes), lambda i: (0, 0)),
            pl.BlockSpec((1, lanes), lambda i: (0, 0)),
        ],
        out_specs=(pl.BlockSpec((tm, 8), lambda i: (i, 0)), pl.BlockSpec((tm, 8), lambda i: (i, 0)),
                   pl.BlockSpec((1, lanes), lambda i: (0, 0))),
        scratch_shapes=[pltpu.VMEM((tm, tm), _BF16), pltpu.VMEM((1, lanes), _F32)],
        compiler_params=_params(("arbitrary",), vmem),
        name="moe_router",
    )(x, g, wr, br)


def _dispatch_kernel(dest_ref, x_ref, g_ref, xs_hbm, hn_ref, sem_ref):
    i = pl.program_id(0)
    n = pl.num_programs(0)
    tm = x_ref.shape[0]
    slot = i % 2

    def row_copy(s, r, dst):
        return pltpu.make_async_copy(hn_ref.at[s, pl.ds(r, 1), :], xs_hbm.at[pl.ds(dst, 1), :], sem_ref.at[s])

    def drain(s):
        def body(r, carry):
            row_copy(s, 0, 0).wait()
            row_copy(s, 0, 0).wait()
            return carry
        lax.fori_loop(0, tm, body, 0, unroll=DMA_LOOP_UNROLL)

    hn_ref[slot] = _rmsnorm(x_ref[...], g_ref[...])

    def issue(r, carry):
        a = TOP_K * (i * tm + r)
        row_copy(slot, r, dest_ref[a]).start(priority=0)
        row_copy(slot, r, dest_ref[a + 1]).start(priority=1)
        return carry

    lax.fori_loop(0, tm, issue, 0, unroll=DMA_LOOP_UNROLL)

    @pl.when(i > 0)
    def _():
        drain(1 - slot)

    @pl.when(i == n - 1)
    def _():
        drain(slot)


def _dispatch(dest, x, g):
    s, d = x.shape
    tm = _tile(s, 256)
    vmem = 6 * _nbytes((tm, d), _F32)
    return pl.pallas_call(
        _dispatch_kernel,
        out_shape=jax.ShapeDtypeStruct((TOP_K * s, d), _F32),
        grid_spec=pltpu.PrefetchScalarGridSpec(
            num_scalar_prefetch=1,
            grid=(s // tm,),
            in_specs=[pl.BlockSpec((tm, d), lambda i, dest: (i, 0)),
                      pl.BlockSpec((1, d), lambda i, dest: (0, 0))],
            out_specs=pl.BlockSpec(memory_space=pl.ANY),
            scratch_shapes=[pltpu.VMEM((2, tm, d), _F32), pltpu.SemaphoreType.DMA((2,))],
        ),
        compiler_params=_params(("arbitrary",), vmem),
        name="moe_dispatch",
    )(dest, x, g)


def _experts_kernel(wb_ref, we_ref, nxt_ref, slot_ref, lo_ref, hi_ref, nw_ref, x_ref, wg_hbm, wu_hbm, wd_hbm, o_ref,
                    wg_f, wu_f, wd_f, wg_s, wu_s, wd_s, sem_ref, *, layer, cast_rows):
    w = pl.program_id(0)

    def weight_copies(e, slot):
        return [pltpu.make_async_copy(hbm.at[layer, e], stage.at[slot], sem_ref.at[slot])
                for hbm, stage in ((wg_hbm, wg_f), (wu_hbm, wu_f), (wd_hbm, wd_f))]

    @pl.when(w < nw_ref[0])
    def _():
        e = we_ref[w]
        b = wb_ref[w]
        slot = slot_ref[w]
        prev = jnp.maximum(w - 1, 0)
        new_expert = (w == 0) | (we_ref[prev] != e)
        new_block = (w == 0) | (wb_ref[prev] != b)

        @pl.when(w == 0)
        def _():
            for copy in weight_copies(e, slot):
                copy.start()

        @pl.when(new_expert)
        def _():
            for copy in weight_copies(e, slot):
                copy.wait()
            nxt = nxt_ref[w]

            @pl.when(nxt >= 0)
            def _():
                for copy in weight_copies(nxt, 1 - slot):
                    copy.start()

            for src, dst in ((wg_f, wg_s), (wu_f, wu_s), (wd_f, wd_s)):
                for r in range(0, dst.shape[0], cast_rows):
                    dst[r:r + cast_rows, :] = src[slot, r:r + cast_rows, :].astype(_BF16)

        tb = x_ref.shape[0]
        x = x_ref[...].astype(_BF16)
        gate = _mm(x, wg_s[...])
        up = _mm(x, wu_s[...])
        hid = gate * (1.0 / (1.0 + jnp.exp(-gate))) * up
        row = b * tb + lax.broadcasted_iota(jnp.int32, (tb, 1), 0)
        hid = jnp.where((row >= lo_ref[e]) & (row < hi_ref[e]), hid, 0.0).astype(_BF16)
        out = _mm(hid, wd_s[...])

        @pl.when(new_block)
        def _():
            o_ref[...] = out

        @pl.when(jnp.logical_not(new_block))
        def _():
            o_ref[...] += out


def _experts(plan, xs, w_gate, w_up, w_down, layer, tb):
    a, d = xs.shape
    f = w_gate.shape[3]
    wb, we, nxt, slot, lo, hi, nw = plan
    cast_rows = _tile(f, 256)
    vmem = (4 * _nbytes((tb, d), _F32) + 6 * _nbytes((d, f), _F32) + 3 * _nbytes((d, f), _BF16)
            + 6 * _nbytes((tb, f), _F32) + 3 * _nbytes((tb, d), _F32) + 2 * _nbytes((cast_rows, d), _F32))

    def row_block(w, wb, *_):
        return (wb[w], 0)

    return pl.pallas_call(
        functools.partial(_experts_kernel, layer=layer, cast_rows=cast_rows),
        out_shape=jax.ShapeDtypeStruct((a, d), _F32),
        grid_spec=pltpu.PrefetchScalarGridSpec(
            num_scalar_prefetch=7,
            grid=(wb.shape[0],),
            in_specs=[
                pl.BlockSpec((tb, d), row_block),
                pl.BlockSpec(memory_space=pl.ANY),
                pl.BlockSpec(memory_space=pl.ANY),
                pl.BlockSpec(memory_space=pl.ANY),
            ],
            out_specs=pl.BlockSpec((tb, d), row_block),
            scratch_shapes=[pltpu.VMEM((2, d, f), _F32), pltpu.VMEM((2, d, f), _F32), pltpu.VMEM((2, f, d), _F32),
                            pltpu.VMEM((d, f), _BF16), pltpu.VMEM((d, f), _BF16), pltpu.VMEM((f, d), _BF16),
                            pltpu.SemaphoreType.DMA((2,))],
        ),
        compiler_params=_params(("arbitrary",), vmem),
        name="moe_experts",
    )(wb, we, nxt, slot, lo, hi, nw, xs, w_gate, w_up, w_down)


def _combine_kernel(dest_ref, x_ref, gates_ref, gf_ref, ys_hbm, o_ref, buf_ref, sem_ref, *, final_norm):
    i = pl.program_id(0)
    n = pl.num_programs(0)
    tm = x_ref.shape[0]
    slot = i % 2

    def row_copy(s, k, r, src):
        return pltpu.make_async_copy(ys_hbm.at[pl.ds(src, 1), :], buf_ref.at[s, k, pl.ds(r, 1), :],
                                     sem_ref.at[s])

    def issue(tile, s):
        def body(r, carry):
            a = TOP_K * (tile * tm + r)
            row_copy(s, 0, r, dest_ref[a]).start(priority=0)
            row_copy(s, 1, r, dest_ref[a + 1]).start(priority=1)
            return carry
        lax.fori_loop(0, tm, body, 0, unroll=DMA_LOOP_UNROLL)

    @pl.when(i == 0)
    def _():
        issue(0, 0)

    @pl.when(i + 1 < n)
    def _():
        issue(i + 1, 1 - slot)

    def drain(r, carry):
        row_copy(slot, 0, 0, 0).wait()
        row_copy(slot, 1, 0, 0).wait()
        return carry

    lax.fori_loop(0, tm, drain, 0, unroll=DMA_LOOP_UNROLL)

    gates = gates_ref[...]
    y = gates[:, 0:1] * buf_ref[slot, 0] + gates[:, 1:2] * buf_ref[slot, 1]
    out = x_ref[...] + y
    if final_norm:
        out = _rmsnorm(out, gf_ref[...])
    o_ref[...] = out


def _combine(dest, x, gates, ys, gf, final_norm):
    s, d = x.shape
    tm = _tile(s, 256)
    vmem = 4 * _nbytes((tm, d), _F32) + 4 * _nbytes((tm, d), _F32) + 6 * _nbytes((tm, d), _F32)
    return pl.pallas_call(
        functools.partial(_combine_kernel, final_norm=final_norm),
        out_shape=jax.ShapeDtypeStruct((s, d), _F32),
        grid_spec=pltpu.PrefetchScalarGridSpec(
            num_scalar_prefetch=1,
            grid=(s // tm,),
            in_specs=[
                pl.BlockSpec((tm, d), lambda i, dest: (i, 0)),
                pl.BlockSpec((tm, gates.shape[1]), lambda i, dest: (i, 0)),
                pl.BlockSpec((1, d), lambda i, dest: (0, 0)),
                pl.BlockSpec(memory_space=pl.ANY),
            ],
            out_specs=pl.BlockSpec((tm, d), lambda i, dest: (i, 0)),
            scratch_shapes=[pltpu.VMEM((2, TOP_K, tm, d), _F32), pltpu.SemaphoreType.DMA((2,))],
        ),
        compiler_params=_params(("arbitrary",), vmem),
        name="moe_combine",
    )(dest, x, gates, gf, ys)


def _moe_plan(ids, counts, n_groups, n_experts, n_rows, tb):
    cnt = counts[0, n_groups:n_groups + n_experts].astype(jnp.int32)
    ends = jnp.cumsum(cnt)
    starts = ends - cnt
    experts = jnp.arange(n_experts, dtype=jnp.int32)

    def lookup(table, idx):
        return jnp.sum(jnp.where(idx[..., None] == experts, table, 0), axis=-1)

    dest = (lookup(starts, ids[:, 0:TOP_K]) + ids[:, TOP_K:2 * TOP_K]).reshape(-1).astype(jnp.int32)

    first_blk = starts // tb
    n_items = jnp.where(cnt > 0, (ends - 1) // tb - first_blk + 1, 0)
    item_end = jnp.cumsum(n_items)
    n_work = item_end[-1]
    max_work = n_rows // tb + n_experts
    w = jnp.minimum(jnp.arange(max_work, dtype=jnp.int32), n_work - 1)
    we = jnp.sum((item_end[None, :] <= w[:, None]).astype(jnp.int32), axis=1)
    wb = (lookup(first_blk - (item_end - n_items), we) + w).astype(jnp.int32)
    later = (experts[None, :] > experts[:, None]) & (cnt[None, :] > 0)
    next_expert = jnp.min(jnp.where(later, experts[None, :], n_experts), axis=1)
    next_expert = jnp.where(next_expert == n_experts, -1, next_expert)
    stage_slot = (jnp.cumsum((cnt > 0).astype(jnp.int32)) - 1) % 2
    plan = (wb, we, lookup(next_expert, we).astype(jnp.int32), lookup(stage_slot, we).astype(jnp.int32),
            starts.astype(jnp.int32), ends.astype(jnp.int32), n_work.reshape(1).astype(jnp.int32))
    return dest, plan


def _hierarchical_moe(x, g, rg_w, rg_b, re_w, re_b, w_gate, w_up, w_down, layer, gf, final_norm):
    s, d = x.shape
    n_groups, _, epg = re_w.shape
    n_experts = n_groups * epg
    lanes = V7X_LANES
    assert n_groups + n_experts <= lanes
    wr = jnp.concatenate([rg_w, jnp.transpose(re_w, (1, 0, 2)).reshape(d, n_experts)], axis=1)
    wr = jnp.pad(wr, ((0, 0), (0, lanes - wr.shape[1])))
    br = jnp.pad(jnp.concatenate([rg_b, re_b.reshape(-1)]), (0, lanes - n_groups - n_experts)).reshape(1, lanes)

    ids, gates, counts = _router(x, g, wr, br, n_groups, epg)
    tb = _tile(TOP_K * s, 256)
    dest, plan = _moe_plan(ids, counts, n_groups, n_experts, TOP_K * s, tb)
    xs = _dispatch(dest, x, g)
    ys = _experts(plan, xs, w_gate, w_up, w_down, layer, tb)
    return _combine(dest, x, gates, ys, gf, final_norm)


def kernel(x, mix_norm, ffn_norm, final_norm, conv_w_in, conv_w, conv_w_out, attn_w_in, attn_b_f, attn_w_out,
           router_group_w, router_group_b, router_expert_w, router_expert_b, w_gate, w_up, w_down):
    b, s, d = x.shape
    assert b == 1 and mix_norm.shape[0] == 2
    n_heads = attn_b_f.shape[-1]
    lanes = V7X_LANES
    gf = final_norm.reshape(1, d)
    h = x.reshape(s, d)

    go, z = _conv_inproj(h, mix_norm[0].reshape(1, d), conv_w_in[0])
    h = _conv_outproj(go, z, conv_w[0], conv_w_out[0], h)
    h = _hierarchical_moe(h, ffn_norm[0].reshape(1, d), router_group_w[0], router_group_b[0],
                          router_expert_w[0], router_expert_b[0], w_gate, w_up, w_down, 0,
                          gf, final_norm=False)

    wf = jnp.pad(attn_w_in[0][:, 3 * d:], ((0, 0), (0, lanes - n_heads)))
    bf = jnp.pad(attn_b_f[0], (0, lanes - n_heads)).reshape(1, lanes)
    qkv, fl = _attn_inproj(h, mix_norm[1].reshape(1, d), attn_w_in[0], wf, bf, n_heads)
    kb = _forget_cumsum(fl, n_heads)
    o = _fox_attention(qkv, kb, n_heads, d)
    h = _matmul_residual(o, attn_w_out[0], h)
    h = _hierarchical_moe(h, ffn_norm[1].reshape(1, d), router_group_w[1], router_group_b[1],
                          router_expert_w[1], router_expert_b[1], w_gate, w_up, w_down, 1,
                          gf, final_norm=True)
    return h.reshape(b, s, d)
```

```python
import functools
import math

import jax
import jax.numpy as jnp
from jax import lax
from jax.experimental import pallas as pl
from jax.experimental.pallas import tpu as pltpu

RMS_EPS = 1e-6
NEG_INF = -1e30
TOP_K = 2
LOG2E = 1.4426950408889634

V7X_LANES = 128
V7X_F32_SUBLANES = 8
V7X_BF16_SUBLANES = 16
V7X_VMEM_BYTES = 64 * 1024 * 1024

_BF16 = jnp.bfloat16
_F32 = jnp.float32


def _tile(n, pref):
    t = min(n, pref)
    while n % t:
        t //= 2
    return t


def _params(semantics, vmem_bytes):
    limit = min(int(vmem_bytes), V7X_VMEM_BYTES - 4 * 1024 * 1024)
    return pltpu.CompilerParams(dimension_semantics=semantics, vmem_limit_bytes=limit)


def _nbytes(shape, dtype):
    return math.prod(shape) * jnp.dtype(dtype).itemsize


def _rmsnorm(x, g):
    ms = jnp.mean(x * x, axis=-1, keepdims=True)
    return x * lax.rsqrt(ms + RMS_EPS) * g


def _mm(a, b):
    return jnp.dot(a, b, preferred_element_type=_F32)


def _conv_inproj_kernel(x_ref, g_ref, wgo_ref, wgi_ref, wu_ref, go_ref, z_ref, hn_ref):
    @pl.when(pl.program_id(1) == 0)
    def _():
        hn_ref[...] = _rmsnorm(x_ref[...], g_ref[...]).astype(_BF16)

    hn = hn_ref[...]
    go_ref[...] = _mm(hn, wgo_ref[...].astype(_BF16)).astype(go_ref.dtype)
    gate_in = _mm(hn, wgi_ref[...].astype(_BF16))
    u = _mm(hn, wu_ref[...].astype(_BF16))
    z_ref[...] = (gate_in * u).astype(z_ref.dtype)


def _conv_inproj(x, g, w_in):
    s, d = x.shape
    tm, tn = _tile(s, 1024), _tile(d, 256)
    nj = d // tn
    vmem = (2 * _nbytes((tm, d), _F32) + _nbytes((tm, d), _BF16) + 6 * _nbytes((d, tn), _F32)
            + 3 * _nbytes((d, tn), _BF16) + 4 * _nbytes((tm, tn), _BF16) + 4 * _nbytes((tm, tn), _F32)
            + 2 * _nbytes((tm, d), _F32))
    return pl.pallas_call(
        _conv_inproj_kernel,
        out_shape=(jax.ShapeDtypeStruct((s, d), _BF16), jax.ShapeDtypeStruct((s, d), _BF16)),
        grid=(s // tm, nj),
        in_specs=[
            pl.BlockSpec((tm, d), lambda i, j: (i, 0)),
            pl.BlockSpec((1, d), lambda i, j: (0, 0)),
            pl.BlockSpec((d, tn), lambda i, j: (0, j)),
            pl.BlockSpec((d, tn), lambda i, j: (0, j + nj)),
            pl.BlockSpec((d, tn), lambda i, j: (0, j + 2 * nj)),
        ],
        out_specs=(pl.BlockSpec((tm, tn), lambda i, j: (i, j)),
                   pl.BlockSpec((tm, tn), lambda i, j: (i, j))),
        scratch_shapes=[pltpu.VMEM((tm, d), _BF16)],
        compiler_params=_params(("parallel", "arbitrary"), vmem),
        name="conv_inproj",
    )(x, g, w_in, w_in, w_in)


def _conv_outproj_kernel(go_ref, z_ref, halo_ref, cw_ref, w_ref, x_ref, o_ref, y_ref, *, chunk):
    i = pl.program_id(0)

    @pl.when(pl.program_id(1) == 0)
    def _():
        tm, d = y_ref.shape
        row = lax.broadcasted_iota(jnp.int32, (tm, chunk), 0)
        keep = (i > 0).astype(_F32)
        last = halo_ref.shape[0] - 1
        for c in range(d // chunk):
            cs = slice(c * chunk, (c + 1) * chunk)
            z = z_ref[:, cs].astype(_F32)
            halo = halo_ref[:, cs].astype(_F32) * keep
            h1 = halo[last:last + 1, :]
            h2 = halo[last - 1:last, :]
            z1 = jnp.where(row == 0, h1, pltpu.roll(z, 1, axis=0))
            z2 = jnp.where(row == 0, h2, jnp.where(row == 1, h1, pltpu.roll(z, 2, axis=0)))
            cw = cw_ref[:, cs]
            conv = cw[0:1, :] * z2 + cw[1:2, :] * z1 + cw[2:3, :] * z
            y_ref[:, cs] = (go_ref[:, cs].astype(_F32) * conv).astype(_BF16)

    o_ref[...] = x_ref[...] + _mm(y_ref[...], w_ref[...].astype(_BF16))


def _conv_outproj(go, z, conv_w, w_out, x):
    s, d = x.shape
    tm, tn = _tile(s, 1024), _tile(d, 512)
    hb = V7X_BF16_SUBLANES
    chunk = _tile(d, 512)
    vmem = (4 * _nbytes((tm, d), _BF16) + _nbytes((tm, d), _BF16) + 2 * _nbytes((d, tn), _F32)
            + _nbytes((d, tn), _BF16) + 6 * _nbytes((tm, tn), _F32) + 8 * _nbytes((tm, chunk), _F32))
    return pl.pallas_call(
        functools.partial(_conv_outproj_kernel, chunk=chunk),
        out_shape=jax.ShapeDtypeStruct((s, d), _F32),
        grid=(s // tm, d // tn),
        in_specs=[
            pl.BlockSpec((tm, d), lambda i, j: (i, 0)),
            pl.BlockSpec((tm, d), lambda i, j: (i, 0)),
            pl.BlockSpec((hb, d), lambda i, j: (jnp.maximum(i * (tm // hb) - 1, 0), 0)),
            pl.BlockSpec((conv_w.shape[0], d), lambda i, j: (0, 0)),
            pl.BlockSpec((d, tn), lambda i, j: (0, j)),
            pl.BlockSpec((tm, tn), lambda i, j: (i, j)),
        ],
        out_specs=pl.BlockSpec((tm, tn), lambda i, j: (i, j)),
        scratch_shapes=[pltpu.VMEM((tm, d), _BF16)],
        compiler_params=_params(("parallel", "arbitrary"), vmem),
        name="conv_outproj",
    )(go, z, z, conv_w, w_out, x)


def _attn_inproj_kernel(x_ref, g_ref, w_ref, wf_ref, bf_ref, qkv_ref, fl_ref, hn_ref, *, n_q_blocks, q_scale):
    j = pl.program_id(1)

    @pl.when(j == 0)
    def _():
        hn = _rmsnorm(x_ref[...], g_ref[...]).astype(_BF16)
        hn_ref[...] = hn
        fl_ref[...] = _mm(hn, wf_ref[...].astype(_BF16)) + bf_ref[...]

    res = _mm(hn_ref[...], w_ref[...].astype(_BF16)) * jnp.where(j < n_q_blocks, q_scale, 1.0)
    res = res.astype(qkv_ref.dtype)
    hd = qkv_ref.shape[2]
    for h in range(qkv_ref.shape[0]):
        qkv_ref[h] = res[:, h * hd:(h + 1) * hd]


def _attn_inproj(x, g, w_in, wf, bf, n_heads):
    s, d = x.shape
    hd = d // n_heads
    tm, tn = _tile(s, 1024), _tile(d, 1024)
    hpb = tn // hd
    kern = functools.partial(_attn_inproj_kernel, n_q_blocks=d // tn, q_scale=LOG2E / math.sqrt(hd))
    vmem = (2 * _nbytes((tm, d), _F32) + _nbytes((tm, d), _BF16) + 2 * _nbytes((d, tn), _F32)
            + _nbytes((d, tn), _BF16) + 3 * _nbytes((tm, tn), _F32) + 2 * _nbytes((tm, d), _F32)
            + 4 * _nbytes((d, V7X_LANES), _F32))
    return pl.pallas_call(
        kern,
        out_shape=(jax.ShapeDtypeStruct((3 * n_heads, s, hd), _BF16),
                   jax.ShapeDtypeStruct((s, V7X_LANES), _F32)),
        grid=(s // tm, 3 * d // tn),
        in_specs=[
            pl.BlockSpec((tm, d), lambda i, j: (i, 0)),
            pl.BlockSpec((1, d), lambda i, j: (0, 0)),
            pl.BlockSpec((d, tn), lambda i, j: (0, j)),
            pl.BlockSpec((d, V7X_LANES), lambda i, j: (0, 0)),
            pl.BlockSpec((1, V7X_LANES), lambda i, j: (0, 0)),
        ],
        out_specs=(pl.BlockSpec((hpb, tm, hd), lambda i, j: (j, i, 0)),
                   pl.BlockSpec((tm, V7X_LANES), lambda i, j: (i, 0))),
        scratch_shapes=[pltpu.VMEM((tm, d), _BF16)],
        compiler_params=_params(("parallel", "arbitrary"), vmem),
        name="attn_inproj",
    )(x, g, w_in, wf, bf)


def _forget_cumsum_kernel(fl_ref, kb_ref, carry_ref, *, n_heads):
    @pl.when(pl.program_id(0) == 0)
    def _():
        carry_ref[...] = jnp.zeros_like(carry_ref)

    fl = fl_ref[...]
    ls = jnp.minimum(fl, 0.0) - jnp.log1p(jnp.exp(-jnp.abs(fl)))
    tc = ls.shape[0]
    row = lax.broadcasted_iota(jnp.int32, ls.shape, 0)
    shift = 1
    while shift < tc:
        ls = jnp.where(row >= shift, ls + pltpu.roll(ls, shift, axis=0), ls)
        shift *= 2
    ls = ls + carry_ref[...]
    carry_ref[...] = ls[tc - 1:tc, :]

    lane = lax.broadcasted_iota(jnp.int32, ls.shape, 1)
    bias = jnp.where(lane < n_heads, ls * (-LOG2E), 0.0)
    hi = bias.astype(_BF16).astype(_F32)
    rem = bias - hi
    mid = rem.astype(_BF16).astype(_F32)
    lo = (rem - mid).astype(_BF16).astype(_F32)
    pieces = hi + pltpu.roll(mid, n_heads, axis=1) + pltpu.roll(lo, 2 * n_heads, axis=1)
    kb_ref[...] = pieces.astype(_BF16)


def _forget_cumsum(fl, n_heads):
    s, lanes = fl.shape
    assert 3 * n_heads <= lanes
    tc = _tile(s, 2048)
    return pl.pallas_call(
        functools.partial(_forget_cumsum_kernel, n_heads=n_heads),
        out_shape=jax.ShapeDtypeStruct((s, lanes), _BF16),
        grid=(s // tc,),
        in_specs=[pl.BlockSpec((tc, lanes), lambda i: (i, 0))],
        out_specs=pl.BlockSpec((tc, lanes), lambda i: (i, 0)),
        scratch_shapes=[pltpu.VMEM((1, lanes), _F32)],
        compiler_params=_params(("arbitrary",), 16 * _nbytes((tc, lanes), _F32)),
        name="forget_cumsum",
    )(fl)


def _fox_attn_kernel(q_ref, k_ref, v_ref, kb_ref, o_ref, m_ref, acc_ref, s_ref, p_ref, *, n_heads, tk, group):
    h = pl.program_id(0)
    i = pl.program_id(1)
    tq, hd = q_ref.shape
    per_q = tq // tk
    lane = lax.broadcasted_iota(jnp.int32, (tq, hd), 1)
    head_sel = jnp.where((lane == h) | (lane == n_heads + h) | (lane == 2 * n_heads + h), 1.0, 0.0)
    q_ext = jnp.concatenate([q_ref[...], head_sel.astype(_BF16)], axis=1)
    ones_col = jnp.where(lax.broadcasted_iota(jnp.int32, (tk, hd), 1) == 0, 1.0, 0.0).astype(_BF16)

    def scores(j, row0=0):
        k0 = pl.multiple_of(j * tk, tk)
        k_ext = jnp.concatenate([k_ref[pl.ds(k0, tk), :], kb_ref[pl.ds(k0, tk), :]], axis=1)
        return lax.dot_general(q_ext[row0:], k_ext, (((1,), (1,)), ((), ())), preferred_element_type=_F32)

    def weighted_values(p, j):
        k0 = pl.multiple_of(j * tk, tk)
        return _mm(p, jnp.concatenate([v_ref[pl.ds(k0, tk), :], ones_col], axis=1))

    def probs(diag, s, m_prev):
        if diag is not None:
            qi = lax.broadcasted_iota(jnp.int32, s.shape, 0)
            ki = lax.broadcasted_iota(jnp.int32, s.shape, 1)
            s = jnp.where(ki <= qi, s, NEG_INF)
        chunks = [s[:, c * hd:(c + 1) * hd] for c in range(tk // hd)]
        m_new = jnp.maximum(m_prev, jnp.max(functools.reduce(jnp.maximum, chunks), axis=-1, keepdims=True))
        alpha = jnp.exp2(m_prev - m_new)
        p = jnp.concatenate([jnp.exp2(c - m_new).astype(_BF16) for c in chunks], axis=1)
        return m_new, jnp.concatenate([alpha, alpha], axis=1), p

    def run(blocks, prefetch, pending, defer):
        m, acc = m_ref[...], acc_ref[...]
        if pending:
            acc = acc + weighted_values(p_ref[...], blocks[0][0] - 1)
        for n, (j, diag) in enumerate(blocks):
            row0 = (diag or 0) * tk
            s = s_ref[...] if n == 0 else scores(j, row0)
            m_low, alpha, p = probs(diag, s, m[row0:])
            if defer and n == len(blocks) - 1:
                acc = alpha * acc
                p_ref[...] = p
            else:
                acc_low = alpha * acc[row0:] + weighted_values(p, j)
                acc = jnp.concatenate([acc[:row0], acc_low], axis=0) if row0 else acc_low
            m = jnp.concatenate([m[:row0], m_low], axis=0) if row0 else m_low
        if prefetch is not None:
            s_ref[...] = scores(prefetch)
        m_ref[...] = m
        acc_ref[...] = acc

    m_ref[...] = jnp.full_like(m_ref, -jnp.inf)
    acc_ref[...] = jnp.zeros_like(acc_ref)
    s_ref[...] = scores(0)
    n_unmasked = per_q * i
    n_main = n_unmasked // group
    left_over = n_unmasked - group * n_main

    def main_group(t):
        return [(group * t + u, None) for u in range(group)]

    @pl.when(n_main > 0)
    def _():
        run(main_group(0), group, pending=False, defer=True)

    def body(t, carry):
        run(main_group(t), group * (t + 1), pending=True, defer=True)
        return carry

    lax.fori_loop(1, n_main, body, 0)
    diagonal = [(n_unmasked + u, u) for u in range(per_q)]
    for extra in range(0, group, per_q):
        blocks = [(n_unmasked - extra + u, None) for u in range(extra)] + diagonal
        for pending in (False, True):
            @pl.when((left_over == extra) & ((n_main > 0) == pending))
            def _(blocks=blocks, pending=pending):
                run(blocks, None, pending=pending, defer=False)

    acc = acc_ref[...]
    o_ref[...] = (acc[:, :hd] / acc[:, hd:hd + 1]).astype(o_ref.dtype)


def _fox_attention(qkv, kb, n_heads, d):
    _, s, hd = qkv.shape
    assert kb.shape == (s, hd)
    tq = _tile(s, 1024)
    tk = _tile(tq, 512)
    group = 4
    assert group % (tq // tk) == 0
    vmem = (10 * _nbytes((s, hd), _BF16) + 12 * _nbytes((tq, 2 * hd), _F32) + 3 * group * _nbytes((tq, tk), _F32))
    return pl.pallas_call(
        functools.partial(_fox_attn_kernel, n_heads=n_heads, tk=tk, group=group),
        out_shape=jax.ShapeDtypeStruct((s, d), _BF16),
        grid=(n_heads, s // tq),
        in_specs=[
            pl.BlockSpec((None, tq, hd), lambda h, i: (h, i, 0)),
            pl.BlockSpec((None, s, hd), lambda h, i: (n_heads + h, 0, 0)),
            pl.BlockSpec((None, s, hd), lambda h, i: (2 * n_heads + h, 0, 0)),
            pl.BlockSpec((s, hd), lambda h, i: (0, 0)),
        ],
        out_specs=pl.BlockSpec((tq, hd), lambda h, i: (i, h)),
        scratch_shapes=[pltpu.VMEM((tq, hd), _F32), pltpu.VMEM((tq, 2 * hd), _F32), pltpu.VMEM((tq, tk), _F32),
                        pltpu.VMEM((tq, tk), _BF16)],
        compiler_params=_params(("parallel", "arbitrary"), vmem),
        name="fox_attention",
    )(qkv, qkv, qkv, kb)


def _matmul_residual_kernel(a_ref, w_ref, x_ref, o_ref):
    o_ref[...] = x_ref[...] + _mm(a_ref[...], w_ref[...].astype(_BF16))


def _matmul_residual(a, w, x):
    s, d = x.shape
    tm, tn = _tile(s, 1024), _tile(d, 1024)
    vmem = (2 * _nbytes((tm, d), _BF16) + 2 * _nbytes((d, tn), _F32) + _nbytes((d, tn), _BF16)
            + 6 * _nbytes((tm, tn), _F32))
    return pl.pallas_call(
        _matmul_residual_kernel,
        out_shape=jax.ShapeDtypeStruct((s, d), _F32),
        grid=(s // tm, d // tn),
        in_specs=[
            pl.BlockSpec((tm, d), lambda i, j: (i, 0)),
            pl.BlockSpec((d, tn), lambda i, j: (0, j)),
            pl.BlockSpec((tm, tn), lambda i, j: (i, j)),
        ],
        out_specs=pl.BlockSpec((tm, tn), lambda i, j: (i, j)),
        compiler_params=_params(("parallel", "arbitrary"), vmem),
        name="attn_outproj",
    )(a, w, x)


def _router_kernel(x_ref, g_ref, wr_ref, br_ref, ids_ref, gates_ref, counts_ref, tril_ref, carry_ref,
                   *, n_groups, epg):
    i = pl.program_id(0)
    tm = x_ref.shape[0]
    lanes = wr_ref.shape[1]

    @pl.when(i == 0)
    def _():
        r = lax.broadcasted_iota(jnp.int32, (tm, tm), 0)
        c = lax.broadcasted_iota(jnp.int32, (tm, tm), 1)
        tril_ref[...] = (c < r).astype(_BF16)
        carry_ref[...] = jnp.zeros_like(carry_ref)

    hn = _rmsnorm(x_ref[...], g_ref[...])
    logits = _mm(hn.astype(_BF16), wr_ref[...].astype(_BF16)) + br_ref[...]
    lane = lax.broadcasted_iota(jnp.int32, (tm, lanes), 1)

    def first_argmax(vals):
        top = jnp.max(vals, axis=-1, keepdims=True)
        idx = jnp.min(jnp.where(vals == top, lane, lanes), axis=-1, keepdims=True)
        return top, idx

    gl = jnp.where(lane < n_groups, logits, -jnp.inf)
    g_top, g_sel = first_argmax(gl)
    p_g = 1.0 / jnp.sum(jnp.exp(gl - g_top), axis=-1, keepdims=True)
    lo = n_groups + g_sel * epg
    el = jnp.where((lane >= lo) & (lane < lo + epg), logits, -jnp.inf)
    v1, i1 = first_argmax(el)
    v2, i2 = first_argmax(jnp.where(lane == i1, -jnp.inf, el))
    t = jnp.exp(v2 - v1)
    den = 1.0 + t
    gate1 = (1.0 / den) * p_g
    gate2 = (t / den) * p_g

    sel1 = lane == i1
    sel2 = lane == i2
    onehot = (sel1 | sel2).astype(_F32)
    prefix = _mm(tril_ref[...], onehot.astype(_BF16)) + carry_ref[...]
    rank1 = jnp.sum(jnp.where(sel1, prefix, 0.0), axis=-1, keepdims=True).astype(jnp.int32)
    rank2 = jnp.sum(jnp.where(sel2, prefix, 0.0), axis=-1, keepdims=True).astype(jnp.int32)
    carry = carry_ref[...] + jnp.sum(onehot, axis=0, keepdims=True)
    carry_ref[...] = carry
    counts_ref[...] = carry

    ids = jnp.where(lane == 0, i1 - n_groups,
                    jnp.where(lane == 1, i2 - n_groups,
                              jnp.where(lane == 2, rank1, jnp.where(lane == 3, rank2, 0))))
    gates = jnp.where(lane == 0, gate1, jnp.where(lane == 1, gate2, 0.0))
    ids_ref[...] = ids[:, :ids_ref.shape[1]]
    gates_ref[...] = gates[:, :gates_ref.shape[1]]


def _router(x, g, wr, br, n_groups, epg):
    s, d = x.shape
    lanes = wr.shape[1]
    tm = _tile(s, 512)
    vmem = (4 * _nbytes((tm, d), _F32) + _nbytes((tm, tm), _BF16) + 4 * _nbytes((d, lanes), _F32)
            + 24 * _nbytes((tm, lanes), _F32) + 4 * _nbytes((tm, tm), jnp.int32))
    return pl.pallas_call(
        functools.partial(_router_kernel, n_groups=n_groups, epg=epg),
        out_shape=(jax.ShapeDtypeStruct((s, 8), jnp.int32), jax.ShapeDtypeStruct((s, 8), _F32),
                   jax.ShapeDtypeStruct((1, lanes), _F32)),
        grid=(s // tm,),
        in_specs=[
            pl.BlockSpec((tm, d), lambda i: (i, 0)),
            pl.BlockSpec((1, d), lambda i: (0, 0)),
            pl.BlockSpec((d, lanes), lambda i: (0, 0)),
            pl.BlockSpec((1, lanes), lambda i: (0, 0)),
        ],
        out_specs=(pl.BlockSpec((tm, 8), lambda i: (i, 0)), pl.BlockSpec((tm, 8), lambda i: (i, 0)),
                   pl.BlockSpec((1, lanes), lambda i: (0, 0))),
        scratch_shapes=[pltpu.VMEM((tm, tm), _BF16), pltpu.VMEM((1, lanes), _F32)],
        compiler_params=_params(("arbitrary",), vmem),
        name="moe_router",
    )(x, g, wr, br)


def _dispatch_kernel(dest_ref, x_ref, g_ref, xs_hbm, hn_ref, sem_ref):
    i = pl.program_id(0)
    n = pl.num_programs(0)
    tm = x_ref.shape[0]
    slot = i % 2

    sub = hn_ref.shape[2]

    def row_copy(s, grp, u, dst):
        return pltpu.make_async_copy(hn_ref.at[s, grp, pl.ds(u, 1), :], xs_hbm.at[pl.ds(dst, 1), :], sem_ref.at[s])

    def drain(s):
        def body(grp, carry):
            for _ in range(TOP_K * sub):
                row_copy(s, 0, 0, 0).wait()
            return carry
        lax.fori_loop(0, tm // sub, body, 0)

    hn_ref[slot] = _rmsnorm(x_ref[...], g_ref[...]).reshape(tm // sub, sub, x_ref.shape[1])

    def issue(grp, carry):
        a = TOP_K * (i * tm + grp * sub)
        for u in range(sub):
            row_copy(slot, grp, u, dest_ref[a + TOP_K * u]).start(priority=0)
            row_copy(slot, grp, u, dest_ref[a + TOP_K * u + 1]).start(priority=1)
        return carry

    lax.fori_loop(0, tm // sub, issue, 0)

    @pl.when(i > 0)
    def _():
        drain(1 - slot)

    @pl.when(i == n - 1)
    def _():
        drain(slot)


def _dispatch(dest, x, g):
    s, d = x.shape
    tm = _tile(s, 256)
    vmem = 6 * _nbytes((tm, d), _F32)
    return pl.pallas_call(
        _dispatch_kernel,
        out_shape=jax.ShapeDtypeStruct((TOP_K * s, d), _F32),
        grid_spec=pltpu.PrefetchScalarGridSpec(
            num_scalar_prefetch=1,
            grid=(s // tm,),
            in_specs=[pl.BlockSpec((tm, d), lambda i, dest: (i, 0)),
                      pl.BlockSpec((1, d), lambda i, dest: (0, 0))],
            out_specs=pl.BlockSpec(memory_space=pl.ANY),
            scratch_shapes=[pltpu.VMEM((2, tm // V7X_F32_SUBLANES, V7X_F32_SUBLANES, d), _F32),
                            pltpu.SemaphoreType.DMA((2,))],
        ),
        compiler_params=_params(("arbitrary",), vmem),
        name="moe_dispatch",
    )(dest, x, g)


def _experts_kernel(wb_ref, we_ref, nxt_ref, slot_ref, lo_ref, hi_ref, nw_ref, x_ref, wg_hbm, wu_hbm, wd_hbm, o_ref,
                    wg_f, wu_f, wd_f, wg_s, wu_s, wd_s, sem_ref, *, layer, cast_rows):
    w = pl.program_id(0)

    def weight_copies(e, slot):
        return [pltpu.make_async_copy(hbm.at[layer, e], stage.at[slot], sem_ref.at[slot])
                for hbm, stage in ((wg_hbm, wg_f), (wu_hbm, wu_f), (wd_hbm, wd_f))]

    @pl.when(w < nw_ref[0])
    def _():
        e = we_ref[w]
        b = wb_ref[w]
        slot = slot_ref[w]
        prev = jnp.maximum(w - 1, 0)
        new_expert = (w == 0) | (we_ref[prev] != e)
        new_block = (w == 0) | (wb_ref[prev] != b)

        @pl.when(w == 0)
        def _():
            for copy in weight_copies(e, slot):
                copy.start()

        @pl.when(new_expert)
        def _():
            for copy in weight_copies(e, slot):
                copy.wait()
            nxt = nxt_ref[w]

            @pl.when(nxt >= 0)
            def _():
                for copy in weight_copies(nxt, 1 - slot):
                    copy.start()

            for src, dst in ((wg_f, wg_s), (wu_f, wu_s), (wd_f, wd_s)):
                for r in range(0, dst.shape[0], cast_rows):
                    dst[r:r + cast_rows, :] = src[slot, r:r + cast_rows, :].astype(_BF16)

        tb = x_ref.shape[0]
        x = x_ref[...].astype(_BF16)
        gate = _mm(x, wg_s[...])
        up = _mm(x, wu_s[...])
        hid = gate * (1.0 / (1.0 + jnp.exp(-gate))) * up
        row = b * tb + lax.broadcasted_iota(jnp.int32, (tb, 1), 0)
        hid = jnp.where((row >= lo_ref[e]) & (row < hi_ref[e]), hid, 0.0).astype(_BF16)
        out = _mm(hid, wd_s[...])

        @pl.when(new_block)
        def _():
            o_ref[...] = out

        @pl.when(jnp.logical_not(new_block))
        def _():
            o_ref[...] += out


def _experts(plan, xs, w_gate, w_up, w_down, layer, tb):
    a, d = xs.shape
    f = w_gate.shape[3]
    wb, we, nxt, slot, lo, hi, nw = plan
    cast_rows = _tile(f, 256)
    vmem = (4 * _nbytes((tb, d), _F32) + 6 * _nbytes((d, f), _F32) + 3 * _nbytes((d, f), _BF16)
            + 6 * _nbytes((tb, f), _F32) + 3 * _nbytes((tb, d), _F32) + 2 * _nbytes((cast_rows, d), _F32))

    def row_block(w, wb, *_):
        return (wb[w], 0)

    return pl.pallas_call(
        functools.partial(_experts_kernel, layer=layer, cast_rows=cast_rows),
        out_shape=jax.ShapeDtypeStruct((a, d), _F32),
        grid_spec=pltpu.PrefetchScalarGridSpec(
            num_scalar_prefetch=7,
            grid=(wb.shape[0],),
            in_specs=[
                pl.BlockSpec((tb, d), row_block),
                pl.BlockSpec(memory_space=pl.ANY),
                pl.BlockSpec(memory_space=pl.ANY),
                pl.BlockSpec(memory_space=pl.ANY),
            ],
            out_specs=pl.BlockSpec((tb, d), row_block),
            scratch_shapes=[pltpu.VMEM((2, d, f), _F32), pltpu.VMEM((2, d, f), _F32), pltpu.VMEM((2, f, d), _F32),
                            pltpu.VMEM((d, f), _BF16), pltpu.VMEM((d, f), _BF16), pltpu.VMEM((f, d), _BF16),
                            pltpu.SemaphoreType.DMA((2,))],
        ),
        compiler_params=_params(("arbitrary",), vmem),
        name="moe_experts",
    )(wb, we, nxt, slot, lo, hi, nw, xs, w_gate, w_up, w_down)


def _combine_kernel(dest_ref, x_ref, gates_ref, gf_ref, ys_hbm, o_ref, buf_ref, sem_ref, *, final_norm):
    i = pl.program_id(0)
    n = pl.num_programs(0)
    tm = x_ref.shape[0]
    slot = i % 2

    sub = buf_ref.shape[3]
    d = x_ref.shape[1]

    def row_copy(s, k, grp, u, src):
        return pltpu.make_async_copy(ys_hbm.at[pl.ds(src, 1), :], buf_ref.at[s, k, grp, pl.ds(u, 1), :],
                                     sem_ref.at[s])

    def issue(tile, s):
        def body(grp, carry):
            a = TOP_K * (tile * tm + grp * sub)
            for u in range(sub):
                row_copy(s, 0, grp, u, dest_ref[a + TOP_K * u]).start(priority=0)
                row_copy(s, 1, grp, u, dest_ref[a + TOP_K * u + 1]).start(priority=1)
            return carry
        lax.fori_loop(0, tm // sub, body, 0)

    @pl.when(i == 0)
    def _():
        issue(0, 0)

    @pl.when(i + 1 < n)
    def _():
        issue(i + 1, 1 - slot)

    def drain(grp, carry):
        for _ in range(sub):
            row_copy(slot, 0, 0, 0, 0).wait()
            row_copy(slot, 1, 0, 0, 0).wait()
        return carry

    lax.fori_loop(0, tm // sub, drain, 0)

    gates = gates_ref[...]
    y = (gates[:, 0:1] * buf_ref[slot, 0].reshape(tm, d) + gates[:, 1:2] * buf_ref[slot, 1].reshape(tm, d))
    out = x_ref[...] + y
    if final_norm:
        out = _rmsnorm(out, gf_ref[...])
    o_ref[...] = out


def _combine(dest, x, gates, ys, gf, final_norm):
    s, d = x.shape
    tm = _tile(s, 256)
    vmem = 4 * _nbytes((tm, d), _F32) + 4 * _nbytes((tm, d), _F32) + 6 * _nbytes((tm, d), _F32)
    return pl.pallas_call(
        functools.partial(_combine_kernel, final_norm=final_norm),
        out_shape=jax.ShapeDtypeStruct((s, d), _F32),
        grid_spec=pltpu.PrefetchScalarGridSpec(
            num_scalar_prefetch=1,
            grid=(s // tm,),
            in_specs=[
                pl.BlockSpec((tm, d), lambda i, dest: (i, 0)),
                pl.BlockSpec((tm, gates.shape[1]), lambda i, dest: (i, 0)),
                pl.BlockSpec((1, d), lambda i, dest: (0, 0)),
                pl.BlockSpec(memory_space=pl.ANY),
            ],
            out_specs=pl.BlockSpec((tm, d), lambda i, dest: (i, 0)),
            scratch_shapes=[pltpu.VMEM((2, TOP_K, tm // V7X_F32_SUBLANES, V7X_F32_SUBLANES, d), _F32),
                            pltpu.SemaphoreType.DMA((2,))],
        ),
        compiler_params=_params(("arbitrary",), vmem),
        name="moe_combine",
    )(dest, x, gates, gf, ys)


def _moe_plan(ids, counts, n_groups, n_experts, n_rows, tb):
    cnt = counts[0, n_groups:n_groups + n_experts].astype(jnp.int32)
    ends = jnp.cumsum(cnt)
    starts = ends - cnt
    experts = jnp.arange(n_experts, dtype=jnp.int32)

    def lookup(table, idx):
        return jnp.sum(jnp.where(idx[..., None] == experts, table, 0), axis=-1)

    dest = (lookup(starts, ids[:, 0:TOP_K]) + ids[:, TOP_K:2 * TOP_K]).reshape(-1).astype(jnp.int32)

    first_blk = starts // tb
    n_items = jnp.where(cnt > 0, (ends - 1) // tb - first_blk + 1, 0)
    item_end = jnp.cumsum(n_items)
    n_work = item_end[-1]
    max_work = n_rows // tb + n_experts
    w = jnp.minimum(jnp.arange(max_work, dtype=jnp.int32), n_work - 1)
    we = jnp.sum((item_end[None, :] <= w[:, None]).astype(jnp.int32), axis=1)
    wb = (lookup(first_blk - (item_end - n_items), we) + w).astype(jnp.int32)
    later = (experts[None, :] > experts[:, None]) & (cnt[None, :] > 0)
    next_expert = jnp.min(jnp.where(later, experts[None, :], n_experts), axis=1)
    next_expert = jnp.where(next_expert == n_experts, -1, next_expert)
    stage_slot = (jnp.cumsum((cnt > 0).astype(jnp.int32)) - 1) % 2
    plan = (wb, we, lookup(next_expert, we).astype(jnp.int32), lookup(stage_slot, we).astype(jnp.int32),
            starts.astype(jnp.int32), ends.astype(jnp.int32), n_work.reshape(1).astype(jnp.int32))
    return dest, plan


def _hierarchical_moe(x, g, rg_w, rg_b, re_w, re_b, w_gate, w_up, w_down, layer, gf, final_norm):
    s, d = x.shape
    n_groups, _, epg = re_w.shape
    n_experts = n_groups * epg
    lanes = V7X_LANES
    assert n_groups + n_experts <= lanes
    wr = jnp.concatenate([rg_w, jnp.transpose(re_w, (1, 0, 2)).reshape(d, n_experts)], axis=1)
    wr = jnp.pad(wr, ((0, 0), (0, lanes - wr.shape[1])))
    br = jnp.pad(jnp.concatenate([rg_b, re_b.reshape(-1)]), (0, lanes - n_groups - n_experts)).reshape(1, lanes)

    ids, gates, counts = _router(x, g, wr, br, n_groups, epg)
    tb = _tile(TOP_K * s, 256)
    dest, plan = _moe_plan(ids, counts, n_groups, n_experts, TOP_K * s, tb)
    xs = _dispatch(dest, x, g)
    ys = _experts(plan, xs, w_gate, w_up, w_down, layer, tb)
    return _combine(dest, x, gates, ys, gf, final_norm)


def kernel(x, mix_norm, ffn_norm, final_norm, conv_w_in, conv_w, conv_w_out, attn_w_in, attn_b_f, attn_w_out,
           router_group_w, router_group_b, router_expert_w, router_expert_b, w_gate, w_up, w_down):
    b, s, d = x.shape
    assert b == 1 and mix_norm.shape[0] == 2
    n_heads = attn_b_f.shape[-1]
    lanes = V7X_LANES
    gf = final_norm.reshape(1, d)
    h = x.reshape(s, d)

    go, z = _conv_inproj(h, mix_norm[0].reshape(1, d), conv_w_in[0])
    h = _conv_outproj(go, z, conv_w[0], conv_w_out[0], h)
    h = _hierarchical_moe(h, ffn_norm[0].reshape(1, d), router_group_w[0], router_group_b[0],
                          router_expert_w[0], router_expert_b[0], w_gate, w_up, w_down, 0,
                          gf, final_norm=False)

    wf = jnp.pad(attn_w_in[0][:, 3 * d:], ((0, 0), (0, lanes - n_heads)))
    bf = jnp.pad(attn_b_f[0], (0, lanes - n_heads)).reshape(1, lanes)
    qkv, fl = _attn_inproj(h, mix_norm[1].reshape(1, d), attn_w_in[0], wf, bf, n_heads)
    kb = _forget_cumsum(fl, n_heads)
    o = _fox_attention(qkv, kb, n_heads, d)
    h = _matmul_residual(o, attn_w_out[0], h)
    h = _hierarchical_moe(h, ffn_norm[1].reshape(1, d), router_group_w[1], router_group_b[1],
                          router_expert_w[1], router_expert_b[1], w_gate, w_up, w_down, 1,
                          gf, final_norm=True)
    return h.reshape(b, s, d)
```

```python
import functools
import math

import jax
import jax.numpy as jnp
from jax import lax
from jax.experimental import pallas as pl
from jax.experimental.pallas import tpu as pltpu

RMS_EPS = 1e-6
NEG_INF = -1e30
TOP_K = 2
LOG2E = 1.4426950408889634

V7X_LANES = 128
V7X_F32_SUBLANES = 8
V7X_BF16_SUBLANES = 16
V7X_VMEM_BYTES = 64 * 1024 * 1024

_BF16 = jnp.bfloat16
_F32 = jnp.float32


def _tile(n, pref):
    t = min(n, pref)
    while n % t:
        t //= 2
    return t


def _params(semantics, vmem_bytes):
    limit = min(int(vmem_bytes), V7X_VMEM_BYTES - 4 * 1024 * 1024)
    return pltpu.CompilerParams(dimension_semantics=semantics, vmem_limit_bytes=limit)


def _nbytes(shape, dtype):
    return math.prod(shape) * jnp.dtype(dtype).itemsize


def _rmsnorm(x, g):
    ms = jnp.mean(x * x, axis=-1, keepdims=True)
    return x * lax.rsqrt(ms + RMS_EPS) * g


def _mm(a, b):
    return jnp.dot(a, b, preferred_element_type=_F32)


def _conv_inproj_kernel(x_ref, g_ref, wgo_ref, wgi_ref, wu_ref, go_ref, z_ref, hn_ref):
    @pl.when(pl.program_id(1) == 0)
    def _():
        hn_ref[...] = _rmsnorm(x_ref[...], g_ref[...]).astype(_BF16)

    hn = hn_ref[...]
    go_ref[...] = _mm(hn, wgo_ref[...].astype(_BF16)).astype(go_ref.dtype)
    gate_in = _mm(hn, wgi_ref[...].astype(_BF16))
    u = _mm(hn, wu_ref[...].astype(_BF16))
    z_ref[...] = (gate_in * u).astype(z_ref.dtype)


def _conv_inproj(x, g, w_in):
    s, d = x.shape
    tm, tn = _tile(s, 1024), _tile(d, 256)
    nj = d // tn
    vmem = (2 * _nbytes((tm, d), _F32) + _nbytes((tm, d), _BF16) + 6 * _nbytes((d, tn), _F32)
            + 3 * _nbytes((d, tn), _BF16) + 4 * _nbytes((tm, tn), _BF16) + 4 * _nbytes((tm, tn), _F32)
            + 2 * _nbytes((tm, d), _F32))
    return pl.pallas_call(
        _conv_inproj_kernel,
        out_shape=(jax.ShapeDtypeStruct((s, d), _BF16), jax.ShapeDtypeStruct((s, d), _BF16)),
        grid=(s // tm, nj),
        in_specs=[
            pl.BlockSpec((tm, d), lambda i, j: (i, 0)),
            pl.BlockSpec((1, d), lambda i, j: (0, 0)),
            pl.BlockSpec((d, tn), lambda i, j: (0, j)),
            pl.BlockSpec((d, tn), lambda i, j: (0, j + nj)),
            pl.BlockSpec((d, tn), lambda i, j: (0, j + 2 * nj)),
        ],
        out_specs=(pl.BlockSpec((tm, tn), lambda i, j: (i, j)),
                   pl.BlockSpec((tm, tn), lambda i, j: (i, j))),
        scratch_shapes=[pltpu.VMEM((tm, d), _BF16)],
        compiler_params=_params(("parallel", "arbitrary"), vmem),
        name="conv_inproj",
    )(x, g, w_in, w_in, w_in)


def _conv_outproj_kernel(go_ref, z_ref, halo_ref, cw_ref, w_ref, x_ref, o_ref, y_ref, *, chunk):
    i = pl.program_id(0)

    @pl.when(pl.program_id(1) == 0)
    def _():
        tm, d = y_ref.shape
        row = lax.broadcasted_iota(jnp.int32, (tm, chunk), 0)
        keep = (i > 0).astype(_F32)
        last = halo_ref.shape[0] - 1
        for c in range(d // chunk):
            cs = slice(c * chunk, (c + 1) * chunk)
            z = z_ref[:, cs].astype(_F32)
            halo = halo_ref[:, cs].astype(_F32) * keep
            h1 = halo[last:last + 1, :]
            h2 = halo[last - 1:last, :]
            z1 = jnp.where(row == 0, h1, pltpu.roll(z, 1, axis=0))
            z2 = jnp.where(row == 0, h2, jnp.where(row == 1, h1, pltpu.roll(z, 2, axis=0)))
            cw = cw_ref[:, cs]
            conv = cw[0:1, :] * z2 + cw[1:2, :] * z1 + cw[2:3, :] * z
            y_ref[:, cs] = (go_ref[:, cs].astype(_F32) * conv).astype(_BF16)

    o_ref[...] = x_ref[...] + _mm(y_ref[...], w_ref[...].astype(_BF16))


def _conv_outproj(go, z, conv_w, w_out, x):
    s, d = x.shape
    tm, tn = _tile(s, 1024), _tile(d, 512)
    hb = V7X_BF16_SUBLANES
    chunk = _tile(d, 512)
    vmem = (4 * _nbytes((tm, d), _BF16) + _nbytes((tm, d), _BF16) + 2 * _nbytes((d, tn), _F32)
            + _nbytes((d, tn), _BF16) + 6 * _nbytes((tm, tn), _F32) + 8 * _nbytes((tm, chunk), _F32))
    return pl.pallas_call(
        functools.partial(_conv_outproj_kernel, chunk=chunk),
        out_shape=jax.ShapeDtypeStruct((s, d), _F32),
        grid=(s // tm, d // tn),
        in_specs=[
            pl.BlockSpec((tm, d), lambda i, j: (i, 0)),
            pl.BlockSpec((tm, d), lambda i, j: (i, 0)),
            pl.BlockSpec((hb, d), lambda i, j: (jnp.maximum(i * (tm // hb) - 1, 0), 0)),
            pl.BlockSpec((conv_w.shape[0], d), lambda i, j: (0, 0)),
            pl.BlockSpec((d, tn), lambda i, j: (0, j)),
            pl.BlockSpec((tm, tn), lambda i, j: (i, j)),
        ],
        out_specs=pl.BlockSpec((tm, tn), lambda i, j: (i, j)),
        scratch_shapes=[pltpu.VMEM((tm, d), _BF16)],
        compiler_params=_params(("parallel", "arbitrary"), vmem),
        name="conv_outproj",
    )(go, z, z, conv_w, w_out, x)


def _attn_inproj_kernel(x_ref, g_ref, w_ref, wf_ref, bf_ref, qkv_ref, fl_ref, hn_ref, *, n_q_blocks, q_scale):
    j = pl.program_id(1)

    @pl.when(j == 0)
    def _():
        hn = _rmsnorm(x_ref[...], g_ref[...]).astype(_BF16)
        hn_ref[...] = hn
        fl_ref[...] = _mm(hn, wf_ref[...].astype(_BF16)) + bf_ref[...]

    res = _mm(hn_ref[...], w_ref[...].astype(_BF16)) * jnp.where(j < n_q_blocks, q_scale, 1.0)
    res = res.astype(qkv_ref.dtype)
    hd = qkv_ref.shape[2]
    for h in range(qkv_ref.shape[0]):
        qkv_ref[h] = res[:, h * hd:(h + 1) * hd]


def _attn_inproj(x, g, w_in, wf, bf, n_heads):
    s, d = x.shape
    hd = d // n_heads
    tm, tn = _tile(s, 1024), _tile(d, 1024)
    hpb = tn // hd
    kern = functools.partial(_attn_inproj_kernel, n_q_blocks=d // tn, q_scale=LOG2E / math.sqrt(hd))
    vmem = (2 * _nbytes((tm, d), _F32) + _nbytes((tm, d), _BF16) + 2 * _nbytes((d, tn), _F32)
            + _nbytes((d, tn), _BF16) + 3 * _nbytes((tm, tn), _F32) + 2 * _nbytes((tm, d), _F32)
            + 4 * _nbytes((d, V7X_LANES), _F32))
    return pl.pallas_call(
        kern,
        out_shape=(jax.ShapeDtypeStruct((3 * n_heads, s, hd), _BF16),
                   jax.ShapeDtypeStruct((s, V7X_LANES), _F32)),
        grid=(s // tm, 3 * d // tn),
        in_specs=[
            pl.BlockSpec((tm, d), lambda i, j: (i, 0)),
            pl.BlockSpec((1, d), lambda i, j: (0, 0)),
            pl.BlockSpec((d, tn), lambda i, j: (0, j)),
            pl.BlockSpec((d, V7X_LANES), lambda i, j: (0, 0)),
            pl.BlockSpec((1, V7X_LANES), lambda i, j: (0, 0)),
        ],
        out_specs=(pl.BlockSpec((hpb, tm, hd), lambda i, j: (j, i, 0)),
                   pl.BlockSpec((tm, V7X_LANES), lambda i, j: (i, 0))),
        scratch_shapes=[pltpu.VMEM((tm, d), _BF16)],
        compiler_params=_params(("parallel", "arbitrary"), vmem),
        name="attn_inproj",
    )(x, g, w_in, wf, bf)


def _forget_cumsum_kernel(fl_ref, kb_ref, carry_ref, *, n_heads):
    @pl.when(pl.program_id(0) == 0)
    def _():
        carry_ref[...] = jnp.zeros_like(carry_ref)

    fl = fl_ref[...]
    ls = jnp.minimum(fl, 0.0) - jnp.log1p(jnp.exp(-jnp.abs(fl)))
    tc = ls.shape[0]
    row = lax.broadcasted_iota(jnp.int32, ls.shape, 0)
    shift = 1
    while shift < tc:
        ls = jnp.where(row >= shift, ls + pltpu.roll(ls, shift, axis=0), ls)
        shift *= 2
    ls = ls + carry_ref[...]
    carry_ref[...] = ls[tc - 1:tc, :]

    lane = lax.broadcasted_iota(jnp.int32, ls.shape, 1)
    bias = jnp.where(lane < n_heads, ls * (-LOG2E), 0.0)
    hi = bias.astype(_BF16).astype(_F32)
    rem = bias - hi
    mid = rem.astype(_BF16).astype(_F32)
    lo = (rem - mid).astype(_BF16).astype(_F32)
    pieces = hi + pltpu.roll(mid, n_heads, axis=1) + pltpu.roll(lo, 2 * n_heads, axis=1)
    kb_ref[...] = pieces.astype(_BF16)


def _forget_cumsum(fl, n_heads):
    s, lanes = fl.shape
    assert 3 * n_heads <= lanes
    tc = _tile(s, 2048)
    return pl.pallas_call(
        functools.partial(_forget_cumsum_kernel, n_heads=n_heads),
        out_shape=jax.ShapeDtypeStruct((s, lanes), _BF16),
        grid=(s // tc,),
        in_specs=[pl.BlockSpec((tc, lanes), lambda i: (i, 0))],
        out_specs=pl.BlockSpec((tc, lanes), lambda i: (i, 0)),
        scratch_shapes=[pltpu.VMEM((1, lanes), _F32)],
        compiler_params=_params(("arbitrary",), 16 * _nbytes((tc, lanes), _F32)),
        name="forget_cumsum",
    )(fl)


NEXT_QUERY_BLOCK = "next query block"


def _fox_attn_kernel(q_ref, k_ref, v_ref, kb_ref, o_ref, m_ref, acc_ref, s_ref, p_ref, *, n_heads, tq, tk, group):
    h = pl.program_id(0)
    i = pl.program_id(1)
    hd = q_ref.shape[1]
    per_q = tq // tk
    lane = lax.broadcasted_iota(jnp.int32, (tq, hd), 1)
    head_sel = jnp.where((lane == h) | (lane == n_heads + h) | (lane == 2 * n_heads + h), 1.0, 0.0).astype(_BF16)
    ones_col = jnp.where(lax.broadcasted_iota(jnp.int32, (tk, hd), 1) == 0, 1.0, 0.0).astype(_BF16)

    def query_block(idx):
        q0 = pl.multiple_of(idx * tq, tq)
        return jnp.concatenate([q_ref[pl.ds(q0, tq), :], head_sel], axis=1)

    q_ext = query_block(i)

    def scores(j, row0=0, q=None):
        k0 = pl.multiple_of(j * tk, tk)
        k_ext = jnp.concatenate([k_ref[pl.ds(k0, tk), :], kb_ref[pl.ds(k0, tk), :]], axis=1)
        q = q_ext if q is None else q
        return lax.dot_general(q[row0:], k_ext, (((1,), (1,)), ((), ())), preferred_element_type=_F32)

    def weighted_values(p, j):
        k0 = pl.multiple_of(j * tk, tk)
        return _mm(p, jnp.concatenate([v_ref[pl.ds(k0, tk), :], ones_col], axis=1))

    def probs(diag, s, m_prev):
        if diag is not None:
            qi = lax.broadcasted_iota(jnp.int32, s.shape, 0)
            ki = lax.broadcasted_iota(jnp.int32, s.shape, 1)
            s = jnp.where(ki <= qi, s, NEG_INF)
        chunks = [s[:, c * hd:(c + 1) * hd] for c in range(tk // hd)]
        m_new = jnp.maximum(m_prev, jnp.max(functools.reduce(jnp.maximum, chunks), axis=-1, keepdims=True))
        alpha = jnp.exp2(m_prev - m_new)
        p = jnp.concatenate([jnp.exp2(c - m_new).astype(_BF16) for c in chunks], axis=1)
        return m_new, jnp.concatenate([alpha, alpha], axis=1), p

    def run(blocks, prefetch, pending, defer):
        m, acc = m_ref[...], acc_ref[...]
        if pending:
            acc = acc + weighted_values(p_ref[...], blocks[0][0] - 1)
        for n, (j, diag) in enumerate(blocks):
            row0 = (diag or 0) * tk
            s = s_ref[...] if n == 0 else scores(j, row0)
            m_low, alpha, p = probs(diag, s, m[row0:])
            if defer and n == len(blocks) - 1:
                acc = alpha * acc
                p_ref[...] = p
            else:
                acc_low = alpha * acc[row0:] + weighted_values(p, j)
                acc = jnp.concatenate([acc[:row0], acc_low], axis=0) if row0 else acc_low
            m = jnp.concatenate([m[:row0], m_low], axis=0) if row0 else m_low
        if prefetch is NEXT_QUERY_BLOCK:
            s_ref[...] = scores(0, q=query_block(jnp.minimum(i + 1, pl.num_programs(1) - 1)))
        elif prefetch is not None:
            s_ref[...] = scores(prefetch)
        m_ref[...] = m
        acc_ref[...] = acc

    m_ref[...] = jnp.full_like(m_ref, -jnp.inf)
    acc_ref[...] = jnp.zeros_like(acc_ref)

    @pl.when(i == 0)
    def _():
        s_ref[...] = scores(0)
    n_unmasked = per_q * i
    n_main = n_unmasked // group
    left_over = n_unmasked - group * n_main

    def main_group(t):
        return [(group * t + u, None) for u in range(group)]

    @pl.when(n_main > 0)
    def _():
        run(main_group(0), group, pending=False, defer=True)

    def body(t, carry):
        run(main_group(t), group * (t + 1), pending=True, defer=True)
        return carry

    lax.fori_loop(1, n_main, body, 0)
    diagonal = [(n_unmasked + u, u) for u in range(per_q)]
    for extra in range(0, group, per_q):
        blocks = [(n_unmasked - extra + u, None) for u in range(extra)] + diagonal
        for pending in (False, True):
            @pl.when((left_over == extra) & ((n_main > 0) == pending))
            def _(blocks=blocks, pending=pending):
                run(blocks, NEXT_QUERY_BLOCK, pending=pending, defer=False)

    acc = acc_ref[...]
    o_ref[...] = (acc[:, :hd] / acc[:, hd:hd + 1]).astype(o_ref.dtype)


def _fox_attention(qkv, kb, n_heads, d):
    _, s, hd = qkv.shape
    assert kb.shape == (s, hd)
    tq = _tile(s, 1024)
    tk = _tile(tq, 512)
    group = 8
    assert group % (tq // tk) == 0
    vmem = (12 * _nbytes((s, hd), _BF16) + 12 * _nbytes((tq, 2 * hd), _F32) + 3 * group * _nbytes((tq, tk), _F32))
    return pl.pallas_call(
        functools.partial(_fox_attn_kernel, n_heads=n_heads, tq=tq, tk=tk, group=group),
        out_shape=jax.ShapeDtypeStruct((s, d), _BF16),
        grid=(n_heads, s // tq),
        in_specs=[
            pl.BlockSpec((None, s, hd), lambda h, i: (h, 0, 0)),
            pl.BlockSpec((None, s, hd), lambda h, i: (n_heads + h, 0, 0)),
            pl.BlockSpec((None, s, hd), lambda h, i: (2 * n_heads + h, 0, 0)),
            pl.BlockSpec((s, hd), lambda h, i: (0, 0)),
        ],
        out_specs=pl.BlockSpec((tq, hd), lambda h, i: (i, h)),
        scratch_shapes=[pltpu.VMEM((tq, hd), _F32), pltpu.VMEM((tq, 2 * hd), _F32), pltpu.VMEM((tq, tk), _F32),
                        pltpu.VMEM((tq, tk), _BF16)],
        compiler_params=_params(("parallel", "arbitrary"), vmem),
        name="fox_attention",
    )(qkv, qkv, qkv, kb)


def _matmul_residual_kernel(a_ref, w_ref, x_ref, o_ref):
    o_ref[...] = x_ref[...] + _mm(a_ref[...], w_ref[...].astype(_BF16))


def _matmul_residual(a, w, x):
    s, d = x.shape
    tm, tn = _tile(s, 1024), _tile(d, 512)
    vmem = (2 * _nbytes((tm, d), _BF16) + 2 * _nbytes((d, tn), _F32) + _nbytes((d, tn), _BF16)
            + 6 * _nbytes((tm, tn), _F32))
    return pl.pallas_call(
        _matmul_residual_kernel,
        out_shape=jax.ShapeDtypeStruct((s, d), _F32),
        grid=(s // tm, d // tn),
        in_specs=[
            pl.BlockSpec((tm, d), lambda i, j: (i, 0)),
            pl.BlockSpec((d, tn), lambda i, j: (0, j)),
            pl.BlockSpec((tm, tn), lambda i, j: (i, j)),
        ],
        out_specs=pl.BlockSpec((tm, tn), lambda i, j: (i, j)),
        compiler_params=_params(("parallel", "arbitrary"), vmem),
        name="attn_outproj",
    )(a, w, x)


def _router_kernel(x_ref, g_ref, wr_ref, br_ref, ids_ref, gates_ref, counts_ref, tril_ref, carry_ref,
                   *, n_groups, epg):
    i = pl.program_id(0)
    tm = x_ref.shape[0]
    lanes = wr_ref.shape[1]

    @pl.when(i == 0)
    def _():
        r = lax.broadcasted_iota(jnp.int32, (tm, tm), 0)
        c = lax.broadcasted_iota(jnp.int32, (tm, tm), 1)
        tril_ref[...] = (c < r).astype(_BF16)
        carry_ref[...] = jnp.zeros_like(carry_ref)

    hn = _rmsnorm(x_ref[...], g_ref[...])
    logits = _mm(hn.astype(_BF16), wr_ref[...].astype(_BF16)) + br_ref[...]
    lane = lax.broadcasted_iota(jnp.int32, (tm, lanes), 1)

    def first_argmax(vals):
        top = jnp.max(vals, axis=-1, keepdims=True)
        idx = jnp.min(jnp.where(vals == top, lane, lanes), axis=-1, keepdims=True)
        return top, idx

    gl = jnp.where(lane < n_groups, logits, -jnp.inf)
    g_top, g_sel = first_argmax(gl)
    p_g = 1.0 / jnp.sum(jnp.exp(gl - g_top), axis=-1, keepdims=True)
    lo = n_groups + g_sel * epg
    el = jnp.where((lane >= lo) & (lane < lo + epg), logits, -jnp.inf)
    v1, i1 = first_argmax(el)
    v2, i2 = first_argmax(jnp.where(lane == i1, -jnp.inf, el))
    t = jnp.exp(v2 - v1)
    den = 1.0 + t
    gate1 = (1.0 / den) * p_g
    gate2 = (t / den) * p_g

    sel1 = lane == i1
    sel2 = lane == i2
    onehot = (sel1 | sel2).astype(_F32)
    prefix = _mm(tril_ref[...], onehot.astype(_BF16)) + carry_ref[...]
    rank1 = jnp.sum(jnp.where(sel1, prefix, 0.0), axis=-1, keepdims=True).astype(jnp.int32)
    rank2 = jnp.sum(jnp.where(sel2, prefix, 0.0), axis=-1, keepdims=True).astype(jnp.int32)
    carry = carry_ref[...] + jnp.sum(onehot, axis=0, keepdims=True)
    carry_ref[...] = carry
    counts_ref[...] = carry

    ids = jnp.where(lane == 0, i1 - n_groups,
                    jnp.where(lane == 1, i2 - n_groups,
                              jnp.where(lane == 2, rank1, jnp.where(lane == 3, rank2, 0))))
    gates = jnp.where(lane == 0, gate1, jnp.where(lane == 1, gate2, 0.0))
    ids_ref[...] = ids[:, :ids_ref.shape[1]]
    gates_ref[...] = gates[:, :gates_ref.shape[1]]


def _router(x, g, wr, br, n_groups, epg):
    s, d = x.shape
    lanes = wr.shape[1]
    tm = _tile(s, 512)
    vmem = (4 * _nbytes((tm, d), _F32) + _nbytes((tm, tm), _BF16) + 4 * _nbytes((d, lanes), _F32)
            + 24 * _nbytes((tm, lanes), _F32) + 4 * _nbytes((tm, tm), jnp.int32))
    return pl.pallas_call(
        functools.partial(_router_kernel, n_groups=n_groups, epg=epg),
        out_shape=(jax.ShapeDtypeStruct((s, 8), jnp.int32), jax.ShapeDtypeStruct((s, 8), _F32),
                   jax.ShapeDtypeStruct((1, lanes), _F32)),
        grid=(s // tm,),
        in_specs=[
            pl.BlockSpec((tm, d), lambda i: (i, 0)),
            pl.BlockSpec((1, d), lambda i: (0, 0)),
            pl.BlockSpec((d, lanes), lambda i: (0, 0)),
            pl.BlockSpec((1, lanes), lambda i: (0, 0)),
        ],
        out_specs=(pl.BlockSpec((tm, 8), lambda i: (i, 0)), pl.BlockSpec((tm, 8), lambda i: (i, 0)),
                   pl.BlockSpec((1, lanes), lambda i: (0, 0))),
        scratch_shapes=[pltpu.VMEM((tm, tm), _BF16), pltpu.VMEM((1, lanes), _F32)],
        compiler_params=_params(("arbitrary",), vmem),
        name="moe_router",
    )(x, g, wr, br)


def _dispatch_kernel(dest_ref, x_ref, g_ref, xs_hbm, hn_ref, sem_ref):
    i = pl.program_id(0)
    n = pl.num_programs(0)
    tm = x_ref.shape[0]
    slot = i % 2

    sub = hn_ref.shape[2]

    def row_copy(s, grp, u, dst):
        return pltpu.make_async_copy(hn_ref.at[s, grp, pl.ds(u, 1), :], xs_hbm.at[pl.ds(dst, 1), :], sem_ref.at[s])

    def drain(s):
        def body(grp, carry):
            for _ in range(TOP_K * sub):
                row_copy(s, 0, 0, 0).wait()
            return carry
        lax.fori_loop(0, tm // sub, body, 0)

    hn_ref[slot] = _rmsnorm(x_ref[...], g_ref[...]).reshape(tm // sub, sub, x_ref.shape[1])

    def issue(grp, carry):
        a = TOP_K * (i * tm + grp * sub)
        for u in range(sub):
            row_copy(slot, grp, u, dest_ref[a + TOP_K * u]).start(priority=0)
            row_copy(slot, grp, u, dest_ref[a + TOP_K * u + 1]).start(priority=1)
        return carry

    lax.fori_loop(0, tm // sub, issue, 0)

    @pl.when(i > 0)
    def _():
        drain(1 - slot)

    @pl.when(i == n - 1)
    def _():
        drain(slot)


def _dispatch(dest, x, g):
    s, d = x.shape
    tm = _tile(s, 256)
    vmem = 6 * _nbytes((tm, d), _F32)
    return pl.pallas_call(
        _dispatch_kernel,
        out_shape=jax.ShapeDtypeStruct((TOP_K * s, d), _F32),
        grid_spec=pltpu.PrefetchScalarGridSpec(
            num_scalar_prefetch=1,
            grid=(s // tm,),
            in_specs=[pl.BlockSpec((tm, d), lambda i, dest: (i, 0)),
                      pl.BlockSpec((1, d), lambda i, dest: (0, 0))],
            out_specs=pl.BlockSpec(memory_space=pl.ANY),
            scratch_shapes=[pltpu.VMEM((2, tm // V7X_F32_SUBLANES, V7X_F32_SUBLANES, d), _F32),
                            pltpu.SemaphoreType.DMA((2,))],
        ),
        compiler_params=_params(("arbitrary",), vmem),
        name="moe_dispatch",
    )(dest, x, g)


def _experts_kernel(wb_ref, we_ref, nxt_ref, slot_ref, lo_ref, hi_ref, nw_ref, x_ref, wg_hbm, wu_hbm, wd_hbm, o_ref,
                    wg_f, wu_f, wd_f, wg_s, wu_s, wd_s, sem_ref, *, layer, cast_rows):
    w = pl.program_id(0)

    def weight_copies(e, slot):
        return [pltpu.make_async_copy(hbm.at[layer, e], stage.at[slot], sem_ref.at[slot])
                for hbm, stage in ((wg_hbm, wg_f), (wu_hbm, wu_f), (wd_hbm, wd_f))]

    @pl.when(w < nw_ref[0])
    def _():
        e = we_ref[w]
        b = wb_ref[w]
        slot = slot_ref[w]
        prev = jnp.maximum(w - 1, 0)
        new_expert = (w == 0) | (we_ref[prev] != e)
        new_block = (w == 0) | (wb_ref[prev] != b)

        @pl.when(w == 0)
        def _():
            for copy in weight_copies(e, slot):
                copy.start()

        @pl.when(new_expert)
        def _():
            for copy in weight_copies(e, slot):
                copy.wait()
            nxt = nxt_ref[w]

            @pl.when(nxt >= 0)
            def _():
                for copy in weight_copies(nxt, 1 - slot):
                    copy.start()

            for src, dst in ((wg_f, wg_s), (wu_f, wu_s), (wd_f, wd_s)):
                for r in range(0, dst.shape[0], cast_rows):
                    dst[r:r + cast_rows, :] = src[slot, r:r + cast_rows, :].astype(_BF16)

        tb = x_ref.shape[0]
        x = x_ref[...].astype(_BF16)
        gate = _mm(x, wg_s[...])
        up = _mm(x, wu_s[...])
        hid = gate * (1.0 / (1.0 + jnp.exp(-gate))) * up
        row = b * tb + lax.broadcasted_iota(jnp.int32, (tb, 1), 0)
        hid = jnp.where((row >= lo_ref[e]) & (row < hi_ref[e]), hid, 0.0).astype(_BF16)
        out = _mm(hid, wd_s[...])

        @pl.when(new_block)
        def _():
            o_ref[...] = out

        @pl.when(jnp.logical_not(new_block))
        def _():
            o_ref[...] += out


def _experts(plan, xs, w_gate, w_up, w_down, layer, tb):
    a, d = xs.shape
    f = w_gate.shape[3]
    wb, we, nxt, slot, lo, hi, nw = plan
    cast_rows = _tile(f, 256)
    vmem = (4 * _nbytes((tb, d), _F32) + 6 * _nbytes((d, f), _F32) + 3 * _nbytes((d, f), _BF16)
            + 6 * _nbytes((tb, f), _F32) + 3 * _nbytes((tb, d), _F32) + 2 * _nbytes((cast_rows, d), _F32))

    def row_block(w, wb, *_):
        return (wb[w], 0)

    return pl.pallas_call(
        functools.partial(_experts_kernel, layer=layer, cast_rows=cast_rows),
        out_shape=jax.ShapeDtypeStruct((a, d), _F32),
        grid_spec=pltpu.PrefetchScalarGridSpec(
            num_scalar_prefetch=7,
            grid=(wb.shape[0],),
            in_specs=[
                pl.BlockSpec((tb, d), row_block),
                pl.BlockSpec(memory_space=pl.ANY),
                pl.BlockSpec(memory_space=pl.ANY),
                pl.BlockSpec(memory_space=pl.ANY),
            ],
            out_specs=pl.BlockSpec((tb, d), row_block),
            scratch_shapes=[pltpu.VMEM((2, d, f), _F32), pltpu.VMEM((2, d, f), _F32), pltpu.VMEM((2, f, d), _F32),
                            pltpu.VMEM((d, f), _BF16), pltpu.VMEM((d, f), _BF16), pltpu.VMEM((f, d), _BF16),
                            pltpu.SemaphoreType.DMA((2,))],
        ),
        compiler_params=_params(("arbitrary",), vmem),
        name="moe_experts",
    )(wb, we, nxt, slot, lo, hi, nw, xs, w_gate, w_up, w_down)


def _combine_kernel(dest_ref, x_ref, gates_ref, gf_ref, ys_hbm, o_ref, buf_ref, sem_ref, *, final_norm):
    i = pl.program_id(0)
    n = pl.num_programs(0)
    tm = x_ref.shape[0]
    slot = i % 2

    sub = buf_ref.shape[3]
    d = x_ref.shape[1]

    def row_copy(s, k, grp, u, src):
        return pltpu.make_async_copy(ys_hbm.at[pl.ds(src, 1), :], buf_ref.at[s, k, grp, pl.ds(u, 1), :],
                                     sem_ref.at[s])

    def issue(tile, s):
        def body(grp, carry):
            a = TOP_K * (tile * tm + grp * sub)
            for u in range(sub):
                row_copy(s, 0, grp, u, dest_ref[a + TOP_K * u]).start(priority=0)
                row_copy(s, 1, grp, u, dest_ref[a + TOP_K * u + 1]).start(priority=1)
            return carry
        lax.fori_loop(0, tm // sub, body, 0)

    @pl.when(i == 0)
    def _():
        issue(0, 0)

    @pl.when(i + 1 < n)
    def _():
        issue(i + 1, 1 - slot)

    def drain(grp, carry):
        for _ in range(sub):
            row_copy(slot, 0, 0, 0, 0).wait()
            row_copy(slot, 1, 0, 0, 0).wait()
        return carry

    lax.fori_loop(0, tm // sub, drain, 0)

    gates = gates_ref[...]
    y = (gates[:, 0:1] * buf_ref[slot, 0].reshape(tm, d) + gates[:, 1:2] * buf_ref[slot, 1].reshape(tm, d))
    out = x_ref[...] + y
    if final_norm:
        out = _rmsnorm(out, gf_ref[...])
    o_ref[...] = out


def _combine(dest, x, gates, ys, gf, final_norm):
    s, d = x.shape
    tm = _tile(s, 256)
    vmem = 4 * _nbytes((tm, d), _F32) + 4 * _nbytes((tm, d), _F32) + 6 * _nbytes((tm, d), _F32)
    return pl.pallas_call(
        functools.partial(_combine_kernel, final_norm=final_norm),
        out_shape=jax.ShapeDtypeStruct((s, d), _F32),
        grid_spec=pltpu.PrefetchScalarGridSpec(
            num_scalar_prefetch=1,
            grid=(s // tm,),
            in_specs=[
                pl.BlockSpec((tm, d), lambda i, dest: (i, 0)),
                pl.BlockSpec((tm, gates.shape[1]), lambda i, dest: (i, 0)),
                pl.BlockSpec((1, d), lambda i, dest: (0, 0)),
                pl.BlockSpec(memory_space=pl.ANY),
            ],
            out_specs=pl.BlockSpec((tm, d), lambda i, dest: (i, 0)),
            scratch_shapes=[pltpu.VMEM((2, TOP_K, tm // V7X_F32_SUBLANES, V7X_F32_SUBLANES, d), _F32),
                            pltpu.SemaphoreType.DMA((2,))],
        ),
        compiler_params=_params(("arbitrary",), vmem),
        name="moe_combine",
    )(dest, x, gates, gf, ys)


def _moe_plan(ids, counts, n_groups, n_experts, n_rows, tb):
    cnt = counts[0, n_groups:n_groups + n_experts].astype(jnp.int32)
    ends = jnp.cumsum(cnt)
    starts = ends - cnt
    experts = jnp.arange(n_experts, dtype=jnp.int32)

    def lookup(table, idx):
        return jnp.sum(jnp.where(idx[..., None] == experts, table, 0), axis=-1)

    dest = (lookup(starts, ids[:, 0:TOP_K]) + ids[:, TOP_K:2 * TOP_K]).reshape(-1).astype(jnp.int32)

    first_blk = starts // tb
    n_items = jnp.where(cnt > 0, (ends - 1) // tb - first_blk + 1, 0)
    item_end = jnp.cumsum(n_items)
    n_work = item_end[-1]
    max_work = n_rows // tb + n_experts
    w = jnp.minimum(jnp.arange(max_work, dtype=jnp.int32), n_work - 1)
    we = jnp.sum((item_end[None, :] <= w[:, None]).astype(jnp.int32), axis=1)
    wb = (lookup(first_blk - (item_end - n_items), we) + w).astype(jnp.int32)
    later = (experts[None, :] > experts[:, None]) & (cnt[None, :] > 0)
    next_expert = jnp.min(jnp.where(later, experts[None, :], n_experts), axis=1)
    next_expert = jnp.where(next_expert == n_experts, -1, next_expert)
    stage_slot = (jnp.cumsum((cnt > 0).astype(jnp.int32)) - 1) % 2
    plan = (wb, we, lookup(next_expert, we).astype(jnp.int32), lookup(stage_slot, we).astype(jnp.int32),
            starts.astype(jnp.int32), ends.astype(jnp.int32), n_work.reshape(1).astype(jnp.int32))
    return dest, plan


def _hierarchical_moe(x, g, rg_w, rg_b, re_w, re_b, w_gate, w_up, w_down, layer, gf, final_norm):
    s, d = x.shape
    n_groups, _, epg = re_w.shape
    n_experts = n_groups * epg
    lanes = V7X_LANES
    assert n_groups + n_experts <= lanes
    wr = jnp.concatenate([rg_w, jnp.transpose(re_w, (1, 0, 2)).reshape(d, n_experts)], axis=1)
    wr = jnp.pad(wr, ((0, 0), (0, lanes - wr.shape[1])))
    br = jnp.pad(jnp.concatenate([rg_b, re_b.reshape(-1)]), (0, lanes - n_groups - n_experts)).reshape(1, lanes)

    ids, gates, counts = _router(x, g, wr, br, n_groups, epg)
    tb = _tile(TOP_K * s, 256)
    dest, plan = _moe_plan(ids, counts, n_groups, n_experts, TOP_K * s, tb)
    xs = _dispatch(dest, x, g)
    ys = _experts(plan, xs, w_gate, w_up, w_down, layer, tb)
    return _combine(dest, x, gates, ys, gf, final_norm)


def kernel(x, mix_norm, ffn_norm, final_norm, conv_w_in, conv_w, conv_w_out, attn_w_in, attn_b_f, attn_w_out,
           router_group_w, router_group_b, router_expert_w, router_expert_b, w_gate, w_up, w_down):
    b, s, d = x.shape
    assert b == 1 and mix_norm.shape[0] == 2
    n_heads = attn_b_f.shape[-1]
    lanes = V7X_LANES
    gf = final_norm.reshape(1, d)
    h = x.reshape(s, d)

    go, z = _conv_inproj(h, mix_norm[0].reshape(1, d), conv_w_in[0])
    h = _conv_outproj(go, z, conv_w[0], conv_w_out[0], h)
    h = _hierarchical_moe(h, ffn_norm[0].reshape(1, d), router_group_w[0], router_group_b[0],
                          router_expert_w[0], router_expert_b[0], w_gate, w_up, w_down, 0,
                          gf, final_norm=False)

    wf = jnp.pad(attn_w_in[0][:, 3 * d:], ((0, 0), (0, lanes - n_heads)))
    bf = jnp.pad(attn_b_f[0], (0, lanes - n_heads)).reshape(1, lanes)
    qkv, fl = _attn_inproj(h, mix_norm[1].reshape(1, d), attn_w_in[0], wf, bf, n_heads)
    kb = _forget_cumsum(fl, n_heads)
    o = _fox_attention(qkv, kb, n_heads, d)
    h = _matmul_residual(o, attn_w_out[0], h)
    h = _hierarchical_moe(h, ffn_norm[1].reshape(1, d), router_group_w[1], router_group_b[1],
                          router_expert_w[1], router_expert_b[1], w_gate, w_up, w_down, 1,
                          gf, final_norm=True)
    return h.reshape(b, s, d)
```

```python
import functools
import math

import jax
import jax.numpy as jnp
from jax import lax
from jax.experimental import pallas as pl
from jax.experimental.pallas import tpu as pltpu

RMS_EPS = 1e-6
NEG_INF = -1e30
TOP_K = 2
LOG2E = 1.4426950408889634

V7X_LANES = 128
V7X_F32_SUBLANES = 8
V7X_BF16_SUBLANES = 16
V7X_VMEM_BYTES = 64 * 1024 * 1024

_BF16 = jnp.bfloat16
_F32 = jnp.float32


def _tile(n, pref):
    t = min(n, pref)
    while n % t:
        t //= 2
    return t


def _params(semantics, vmem_bytes):
    limit = min(int(vmem_bytes), V7X_VMEM_BYTES - 4 * 1024 * 1024)
    return pltpu.CompilerParams(dimension_semantics=semantics, vmem_limit_bytes=limit)


def _nbytes(shape, dtype):
    return math.prod(shape) * jnp.dtype(dtype).itemsize


def _rmsnorm(x, g):
    ms = jnp.mean(x * x, axis=-1, keepdims=True)
    return x * lax.rsqrt(ms + RMS_EPS) * g


def _mm(a, b):
    return jnp.dot(a, b, preferred_element_type=_F32)


def _conv_inproj_kernel(x_ref, g_ref, wgo_ref, wgi_ref, wu_ref, go_ref, z_ref, hn_ref):
    @pl.when(pl.program_id(1) == 0)
    def _():
        hn_ref[...] = _rmsnorm(x_ref[...], g_ref[...]).astype(_BF16)

    hn = hn_ref[...]
    go_ref[...] = _mm(hn, wgo_ref[...].astype(_BF16)).astype(go_ref.dtype)
    gate_in = _mm(hn, wgi_ref[...].astype(_BF16))
    u = _mm(hn, wu_ref[...].astype(_BF16))
    z_ref[...] = (gate_in * u).astype(z_ref.dtype)


def _conv_inproj(x, g, w_in):
    s, d = x.shape
    tm, tn = _tile(s, 1024), _tile(d, 256)
    nj = d // tn
    vmem = (2 * _nbytes((tm, d), _F32) + _nbytes((tm, d), _BF16) + 6 * _nbytes((d, tn), _F32)
            + 3 * _nbytes((d, tn), _BF16) + 4 * _nbytes((tm, tn), _BF16) + 4 * _nbytes((tm, tn), _F32)
            + 2 * _nbytes((tm, d), _F32))
    return pl.pallas_call(
        _conv_inproj_kernel,
        out_shape=(jax.ShapeDtypeStruct((s, d), _BF16), jax.ShapeDtypeStruct((s, d), _BF16)),
        grid=(s // tm, nj),
        in_specs=[
            pl.BlockSpec((tm, d), lambda i, j: (i, 0)),
            pl.BlockSpec((1, d), lambda i, j: (0, 0)),
            pl.BlockSpec((d, tn), lambda i, j: (0, j)),
            pl.BlockSpec((d, tn), lambda i, j: (0, j + nj)),
            pl.BlockSpec((d, tn), lambda i, j: (0, j + 2 * nj)),
        ],
        out_specs=(pl.BlockSpec((tm, tn), lambda i, j: (i, j)),
                   pl.BlockSpec((tm, tn), lambda i, j: (i, j))),
        scratch_shapes=[pltpu.VMEM((tm, d), _BF16)],
        compiler_params=_params(("parallel", "arbitrary"), vmem),
        name="conv_inproj",
    )(x, g, w_in, w_in, w_in)


def _conv_outproj_kernel(go_ref, z_ref, halo_ref, cw_ref, w_ref, x_ref, o_ref, y_ref, *, chunk):
    i = pl.program_id(0)

    @pl.when(pl.program_id(1) == 0)
    def _():
        tm, d = y_ref.shape
        row = lax.broadcasted_iota(jnp.int32, (tm, chunk), 0)
        keep = (i > 0).astype(_F32)
        last = halo_ref.shape[0] - 1
        for c in range(d // chunk):
            cs = slice(c * chunk, (c + 1) * chunk)
            z = z_ref[:, cs].astype(_F32)
            halo = halo_ref[:, cs].astype(_F32) * keep
            h1 = halo[last:last + 1, :]
            h2 = halo[last - 1:last, :]
            z1 = jnp.where(row == 0, h1, pltpu.roll(z, 1, axis=0))
            z2 = jnp.where(row == 0, h2, jnp.where(row == 1, h1, pltpu.roll(z, 2, axis=0)))
            cw = cw_ref[:, cs]
            conv = cw[0:1, :] * z2 + cw[1:2, :] * z1 + cw[2:3, :] * z
            y_ref[:, cs] = (go_ref[:, cs].astype(_F32) * conv).astype(_BF16)

    o_ref[...] = x_ref[...] + _mm(y_ref[...], w_ref[...].astype(_BF16))


def _conv_outproj(go, z, conv_w, w_out, x):
    s, d = x.shape
    tm, tn = _tile(s, 1024), _tile(d, 512)
    hb = V7X_BF16_SUBLANES
    chunk = _tile(d, 512)
    vmem = (4 * _nbytes((tm, d), _BF16) + _nbytes((tm, d), _BF16) + 2 * _nbytes((d, tn), _F32)
            + _nbytes((d, tn), _BF16) + 6 * _nbytes((tm, tn), _F32) + 8 * _nbytes((tm, chunk), _F32))
    return pl.pallas_call(
        functools.partial(_conv_outproj_kernel, chunk=chunk),
        out_shape=jax.ShapeDtypeStruct((s, d), _F32),
        grid=(s // tm, d // tn),
        in_specs=[
            pl.BlockSpec((tm, d), lambda i, j: (i, 0)),
            pl.BlockSpec((tm, d), lambda i, j: (i, 0)),
            pl.BlockSpec((hb, d), lambda i, j: (jnp.maximum(i * (tm // hb) - 1, 0), 0)),
            pl.BlockSpec((conv_w.shape[0], d), lambda i, j: (0, 0)),
            pl.BlockSpec((d, tn), lambda i, j: (0, j)),
            pl.BlockSpec((tm, tn), lambda i, j: (i, j)),
        ],
        out_specs=pl.BlockSpec((tm, tn), lambda i, j: (i, j)),
        scratch_shapes=[pltpu.VMEM((tm, d), _BF16)],
        compiler_params=_params(("parallel", "arbitrary"), vmem),
        name="conv_outproj",
    )(go, z, z, conv_w, w_out, x)


def _attn_inproj_kernel(x_ref, g_ref, w_ref, wf_ref, bf_ref, qkv_ref, fl_ref, hn_ref, *, n_q_blocks, q_scale):
    j = pl.program_id(1)

    @pl.when(j == 0)
    def _():
        hn = _rmsnorm(x_ref[...], g_ref[...]).astype(_BF16)
        hn_ref[...] = hn
        fl_ref[...] = _mm(hn, wf_ref[...].astype(_BF16)) + bf_ref[...]

    res = _mm(hn_ref[...], w_ref[...].astype(_BF16)) * jnp.where(j < n_q_blocks, q_scale, 1.0)
    res = res.astype(qkv_ref.dtype)
    hd = qkv_ref.shape[2]
    for h in range(qkv_ref.shape[0]):
        qkv_ref[h] = res[:, h * hd:(h + 1) * hd]


def _attn_inproj(x, g, w_in, wf, bf, n_heads):
    s, d = x.shape
    hd = d // n_heads
    tm, tn = _tile(s, 1024), _tile(d, 1024)
    hpb = tn // hd
    kern = functools.partial(_attn_inproj_kernel, n_q_blocks=d // tn, q_scale=LOG2E / math.sqrt(hd))
    vmem = (2 * _nbytes((tm, d), _F32) + _nbytes((tm, d), _BF16) + 2 * _nbytes((d, tn), _F32)
            + _nbytes((d, tn), _BF16) + 3 * _nbytes((tm, tn), _F32) + 2 * _nbytes((tm, d), _F32)
            + 4 * _nbytes((d, V7X_LANES), _F32))
    return pl.pallas_call(
        kern,
        out_shape=(jax.ShapeDtypeStruct((3 * n_heads, s, hd), _BF16),
                   jax.ShapeDtypeStruct((s, V7X_LANES), _F32)),
        grid=(s // tm, 3 * d // tn),
        in_specs=[
            pl.BlockSpec((tm, d), lambda i, j: (i, 0)),
            pl.BlockSpec((1, d), lambda i, j: (0, 0)),
            pl.BlockSpec((d, tn), lambda i, j: (0, j)),
            pl.BlockSpec((d, V7X_LANES), lambda i, j: (0, 0)),
            pl.BlockSpec((1, V7X_LANES), lambda i, j: (0, 0)),
        ],
        out_specs=(pl.BlockSpec((hpb, tm, hd), lambda i, j: (j, i, 0)),
                   pl.BlockSpec((tm, V7X_LANES), lambda i, j: (i, 0))),
        scratch_shapes=[pltpu.VMEM((tm, d), _BF16)],
        compiler_params=_params(("parallel", "arbitrary"), vmem),
        name="attn_inproj",
    )(x, g, w_in, wf, bf)


def _forget_cumsum_kernel(fl_ref, kb_ref, carry_ref, *, n_heads):
    @pl.when(pl.program_id(0) == 0)
    def _():
        carry_ref[...] = jnp.zeros_like(carry_ref)

    fl = fl_ref[...]
    ls = jnp.minimum(fl, 0.0) - jnp.log1p(jnp.exp(-jnp.abs(fl)))
    tc = ls.shape[0]
    row = lax.broadcasted_iota(jnp.int32, ls.shape, 0)
    shift = 1
    while shift < tc:
        ls = jnp.where(row >= shift, ls + pltpu.roll(ls, shift, axis=0), ls)
        shift *= 2
    ls = ls + carry_ref[...]
    carry_ref[...] = ls[tc - 1:tc, :]

    lane = lax.broadcasted_iota(jnp.int32, ls.shape, 1)
    bias = jnp.where(lane < n_heads, ls * (-LOG2E), 0.0)
    hi = bias.astype(_BF16).astype(_F32)
    rem = bias - hi
    mid = rem.astype(_BF16).astype(_F32)
    lo = (rem - mid).astype(_BF16).astype(_F32)
    pieces = hi + pltpu.roll(mid, n_heads, axis=1) + pltpu.roll(lo, 2 * n_heads, axis=1)
    kb_ref[...] = pieces.astype(_BF16)


def _forget_cumsum(fl, n_heads):
    s, lanes = fl.shape
    assert 3 * n_heads <= lanes
    tc = _tile(s, 2048)
    return pl.pallas_call(
        functools.partial(_forget_cumsum_kernel, n_heads=n_heads),
        out_shape=jax.ShapeDtypeStruct((s, lanes), _BF16),
        grid=(s // tc,),
        in_specs=[pl.BlockSpec((tc, lanes), lambda i: (i, 0))],
        out_specs=pl.BlockSpec((tc, lanes), lambda i: (i, 0)),
        scratch_shapes=[pltpu.VMEM((1, lanes), _F32)],
        compiler_params=_params(("arbitrary",), 16 * _nbytes((tc, lanes), _F32)),
        name="forget_cumsum",
    )(fl)


NEXT_QUERY_BLOCK = "next query block"


def _fox_attn_kernel(q_ref, k_ref, v_ref, kb_ref, o_ref, m_ref, acc_ref, s_ref, p_ref, *, n_heads, tq, tk, group):
    h = pl.program_id(0)
    i = pl.program_id(1)
    hd = q_ref.shape[1]
    per_q = tq // tk
    lane = lax.broadcasted_iota(jnp.int32, (tq, hd), 1)
    head_sel = jnp.where((lane == h) | (lane == n_heads + h) | (lane == 2 * n_heads + h), 1.0, 0.0).astype(_BF16)
    ones_col = jnp.where(lax.broadcasted_iota(jnp.int32, (tk, hd), 1) == 0, 1.0, 0.0).astype(_BF16)

    def query_block(idx):
        q0 = pl.multiple_of(idx * tq, tq)
        return jnp.concatenate([q_ref[pl.ds(q0, tq), :], head_sel], axis=1)

    q_ext = query_block(i)

    def scores(j, row0=0, q=None):
        k0 = pl.multiple_of(j * tk, tk)
        k_ext = jnp.concatenate([k_ref[pl.ds(k0, tk), :], kb_ref[pl.ds(k0, tk), :]], axis=1)
        q = q_ext if q is None else q
        return lax.dot_general(q[row0:], k_ext, (((1,), (1,)), ((), ())), preferred_element_type=_F32)

    def weighted_values(p, j):
        k0 = pl.multiple_of(j * tk, tk)
        return _mm(p, jnp.concatenate([v_ref[pl.ds(k0, tk), :], ones_col], axis=1))

    def probs(diag, s, m_prev):
        if diag is not None:
            qi = lax.broadcasted_iota(jnp.int32, s.shape, 0)
            ki = lax.broadcasted_iota(jnp.int32, s.shape, 1)
            s = jnp.where(ki <= qi, s, NEG_INF)
        chunks = [s[:, c * hd:(c + 1) * hd] for c in range(tk // hd)]
        m_new = jnp.maximum(m_prev, jnp.max(functools.reduce(jnp.maximum, chunks), axis=-1, keepdims=True))
        alpha = jnp.exp2(m_prev - m_new)
        p = jnp.concatenate([jnp.exp2(c - m_new).astype(_BF16) for c in chunks], axis=1)
        return m_new, jnp.concatenate([alpha, alpha], axis=1), p

    def run(blocks, prefetch, pending, defer):
        m, acc = m_ref[...], acc_ref[...]
        if pending:
            acc = acc + weighted_values(p_ref[...], blocks[0][0] - 1)
        for n, (j, diag) in enumerate(blocks):
            row0 = (diag or 0) * tk
            s = s_ref[...] if n == 0 else scores(j, row0)
            m_low, alpha, p = probs(diag, s, m[row0:])
            if defer and n == len(blocks) - 1:
                acc = alpha * acc
                p_ref[...] = p
            else:
                acc_low = alpha * acc[row0:] + weighted_values(p, j)
                acc = jnp.concatenate([acc[:row0], acc_low], axis=0) if row0 else acc_low
            m = jnp.concatenate([m[:row0], m_low], axis=0) if row0 else m_low
        if prefetch is NEXT_QUERY_BLOCK:
            s_ref[...] = scores(0, q=query_block(jnp.minimum(i + 1, pl.num_programs(1) - 1)))
        elif prefetch is not None:
            s_ref[...] = scores(prefetch)
        m_ref[...] = m
        acc_ref[...] = acc

    m_ref[...] = jnp.full_like(m_ref, -jnp.inf)
    acc_ref[...] = jnp.zeros_like(acc_ref)

    @pl.when(i == 0)
    def _():
        s_ref[...] = scores(0)
    n_unmasked = per_q * i
    n_main = n_unmasked // group
    left_over = n_unmasked - group * n_main

    def main_group(t):
        return [(group * t + u, None) for u in range(group)]

    @pl.when(n_main > 0)
    def _():
        run(main_group(0), group, pending=False, defer=True)

    def body(t, carry):
        run(main_group(t), group * (t + 1), pending=True, defer=True)
        return carry

    lax.fori_loop(1, n_main, body, 0)
    diagonal = [(n_unmasked + u, u) for u in range(per_q)]
    for extra in range(0, group, per_q):
        blocks = [(n_unmasked - extra + u, None) for u in range(extra)] + diagonal
        for pending in (False, True):
            @pl.when((left_over == extra) & ((n_main > 0) == pending))
            def _(blocks=blocks, pending=pending):
                run(blocks, NEXT_QUERY_BLOCK, pending=pending, defer=False)

    acc = acc_ref[...]
    o_ref[...] = (acc[:, :hd] / acc[:, hd:hd + 1]).astype(o_ref.dtype)


def _fox_attention(qkv, kb, n_heads, d):
    _, s, hd = qkv.shape
    assert kb.shape == (s, hd)
    tq = _tile(s, 1024)
    tk = _tile(tq, 512)
    group = 8
    assert group % (tq // tk) == 0
    vmem = (12 * _nbytes((s, hd), _BF16) + 12 * _nbytes((tq, 2 * hd), _F32) + 3 * group * _nbytes((tq, tk), _F32))
    return pl.pallas_call(
        functools.partial(_fox_attn_kernel, n_heads=n_heads, tq=tq, tk=tk, group=group),
        out_shape=jax.ShapeDtypeStruct((s, d), _BF16),
        grid=(n_heads, s // tq),
        in_specs=[
            pl.BlockSpec((None, s, hd), lambda h, i: (h, 0, 0)),
            pl.BlockSpec((None, s, hd), lambda h, i: (n_heads + h, 0, 0)),
            pl.BlockSpec((None, s, hd), lambda h, i: (2 * n_heads + h, 0, 0)),
            pl.BlockSpec((s, hd), lambda h, i: (0, 0)),
        ],
        out_specs=pl.BlockSpec((tq, hd), lambda h, i: (i, h)),
        scratch_shapes=[pltpu.VMEM((tq, hd), _F32), pltpu.VMEM((tq, 2 * hd), _F32), pltpu.VMEM((tq, tk), _F32),
                        pltpu.VMEM((tq, tk), _BF16)],
        compiler_params=_params(("parallel", "arbitrary"), vmem),
        name="fox_attention",
    )(qkv, qkv, qkv, kb)


def _matmul_residual_kernel(a_ref, w_ref, x_ref, o_ref):
    o_ref[...] = x_ref[...] + _mm(a_ref[...], w_ref[...].astype(_BF16))


def _matmul_residual(a, w, x):
    s, d = x.shape
    tm, tn = _tile(s, 2048), _tile(d, 512)
    vmem = (2 * _nbytes((tm, d), _BF16) + 2 * _nbytes((d, tn), _F32) + _nbytes((d, tn), _BF16)
            + 6 * _nbytes((tm, tn), _F32))
    return pl.pallas_call(
        _matmul_residual_kernel,
        out_shape=jax.ShapeDtypeStruct((s, d), _F32),
        grid=(s // tm, d // tn),
        in_specs=[
            pl.BlockSpec((tm, d), lambda i, j: (i, 0)),
            pl.BlockSpec((d, tn), lambda i, j: (0, j)),
            pl.BlockSpec((tm, tn), lambda i, j: (i, j)),
        ],
        out_specs=pl.BlockSpec((tm, tn), lambda i, j: (i, j)),
        compiler_params=_params(("parallel", "arbitrary"), vmem),
        name="attn_outproj",
    )(a, w, x)


def _router_kernel(x_ref, g_ref, wr_ref, br_ref, ids_ref, gates_ref, counts_ref, tril_ref, carry_ref,
                   *, n_groups, epg):
    i = pl.program_id(0)
    tm = x_ref.shape[0]
    lanes = wr_ref.shape[1]

    @pl.when(i == 0)
    def _():
        r = lax.broadcasted_iota(jnp.int32, (tm, tm), 0)
        c = lax.broadcasted_iota(jnp.int32, (tm, tm), 1)
        tril_ref[...] = (c < r).astype(_BF16)
        carry_ref[...] = jnp.zeros_like(carry_ref)

    hn = _rmsnorm(x_ref[...], g_ref[...])
    logits = _mm(hn.astype(_BF16), wr_ref[...].astype(_BF16)) + br_ref[...]
    lane = lax.broadcasted_iota(jnp.int32, (tm, lanes), 1)

    def first_argmax(vals):
        top = jnp.max(vals, axis=-1, keepdims=True)
        idx = jnp.min(jnp.where(vals == top, lane, lanes), axis=-1, keepdims=True)
        return top, idx

    gl = jnp.where(lane < n_groups, logits, -jnp.inf)
    g_top, g_sel = first_argmax(gl)
    p_g = 1.0 / jnp.sum(jnp.exp(gl - g_top), axis=-1, keepdims=True)
    lo = n_groups + g_sel * epg
    el = jnp.where((lane >= lo) & (lane < lo + epg), logits, -jnp.inf)
    v1, i1 = first_argmax(el)
    v2, i2 = first_argmax(jnp.where(lane == i1, -jnp.inf, el))
    t = jnp.exp(v2 - v1)
    den = 1.0 + t
    gate1 = (1.0 / den) * p_g
    gate2 = (t / den) * p_g

    sel1 = lane == i1
    sel2 = lane == i2
    onehot = (sel1 | sel2).astype(_F32)
    prefix = _mm(tril_ref[...], onehot.astype(_BF16)) + carry_ref[...]
    rank1 = jnp.sum(jnp.where(sel1, prefix, 0.0), axis=-1, keepdims=True).astype(jnp.int32)
    rank2 = jnp.sum(jnp.where(sel2, prefix, 0.0), axis=-1, keepdims=True).astype(jnp.int32)
    carry = carry_ref[...] + jnp.sum(onehot, axis=0, keepdims=True)
    carry_ref[...] = carry
    counts_ref[...] = carry

    ids = jnp.where(lane == 0, i1 - n_groups,
                    jnp.where(lane == 1, i2 - n_groups,
                              jnp.where(lane == 2, rank1, jnp.where(lane == 3, rank2, 0))))
    gates = jnp.where(lane == 0, gate1, jnp.where(lane == 1, gate2, 0.0))
    ids_ref[...] = ids[:, :ids_ref.shape[1]]
    gates_ref[...] = gates[:, :gates_ref.shape[1]]


def _router(x, g, wr, br, n_groups, epg):
    s, d = x.shape
    lanes = wr.shape[1]
    tm = _tile(s, 512)
    vmem = (4 * _nbytes((tm, d), _F32) + _nbytes((tm, tm), _BF16) + 4 * _nbytes((d, lanes), _F32)
            + 24 * _nbytes((tm, lanes), _F32) + 4 * _nbytes((tm, tm), jnp.int32))
    return pl.pallas_call(
        functools.partial(_router_kernel, n_groups=n_groups, epg=epg),
        out_shape=(jax.ShapeDtypeStruct((s, 8), jnp.int32), jax.ShapeDtypeStruct((s, 8), _F32),
                   jax.ShapeDtypeStruct((1, lanes), _F32)),
        grid=(s // tm,),
        in_specs=[
            pl.BlockSpec((tm, d), lambda i: (i, 0)),
            pl.BlockSpec((1, d), lambda i: (0, 0)),
            pl.BlockSpec((d, lanes), lambda i: (0, 0)),
            pl.BlockSpec((1, lanes), lambda i: (0, 0)),
        ],
        out_specs=(pl.BlockSpec((tm, 8), lambda i: (i, 0)), pl.BlockSpec((tm, 8), lambda i: (i, 0)),
                   pl.BlockSpec((1, lanes), lambda i: (0, 0))),
        scratch_shapes=[pltpu.VMEM((tm, tm), _BF16), pltpu.VMEM((1, lanes), _F32)],
        compiler_params=_params(("arbitrary",), vmem),
        name="moe_router",
    )(x, g, wr, br)


def _dispatch_kernel(dest_ref, x_ref, g_ref, xs_hbm, hn_ref, sem_ref):
    i = pl.program_id(0)
    n = pl.num_programs(0)
    tm = x_ref.shape[0]
    slot = i % 2

    sub = hn_ref.shape[2]

    def row_copy(s, grp, u, dst):
        return pltpu.make_async_copy(hn_ref.at[s, grp, pl.ds(u, 1), :], xs_hbm.at[pl.ds(dst, 1), :], sem_ref.at[s])

    def drain(s):
        def body(grp, carry):
            for _ in range(TOP_K * sub):
                row_copy(s, 0, 0, 0).wait()
            return carry
        lax.fori_loop(0, tm // sub, body, 0)

    hn_ref[slot] = _rmsnorm(x_ref[...], g_ref[...]).reshape(tm // sub, sub, x_ref.shape[1])

    def issue(grp, carry):
        a = TOP_K * (i * tm + grp * sub)
        for u in range(sub):
            row_copy(slot, grp, u, dest_ref[a + TOP_K * u]).start(priority=0)
            row_copy(slot, grp, u, dest_ref[a + TOP_K * u + 1]).start(priority=1)
        return carry

    lax.fori_loop(0, tm // sub, issue, 0)

    @pl.when(i > 0)
    def _():
        drain(1 - slot)

    @pl.when(i == n - 1)
    def _():
        drain(slot)


def _dispatch(dest, x, g):
    s, d = x.shape
    tm = _tile(s, 512)
    vmem = 6 * _nbytes((tm, d), _F32)
    return pl.pallas_call(
        _dispatch_kernel,
        out_shape=jax.ShapeDtypeStruct((TOP_K * s, d), _F32),
        grid_spec=pltpu.PrefetchScalarGridSpec(
            num_scalar_prefetch=1,
            grid=(s // tm,),
            in_specs=[pl.BlockSpec((tm, d), lambda i, dest: (i, 0)),
                      pl.BlockSpec((1, d), lambda i, dest: (0, 0))],
            out_specs=pl.BlockSpec(memory_space=pl.ANY),
            scratch_shapes=[pltpu.VMEM((2, tm // V7X_F32_SUBLANES, V7X_F32_SUBLANES, d), _F32),
                            pltpu.SemaphoreType.DMA((2,))],
        ),
        compiler_params=_params(("arbitrary",), vmem),
        name="moe_dispatch",
    )(dest, x, g)


def _experts_kernel(wb_ref, we_ref, nxt_ref, slot_ref, lo_ref, hi_ref, nw_ref, x_ref, wg_hbm, wu_hbm, wd_hbm, o_ref,
                    wg_f, wu_f, wd_f, wg_s, wu_s, wd_s, sem_ref, *, layer, cast_rows):
    w = pl.program_id(0)

    def weight_copies(e, slot):
        return [pltpu.make_async_copy(hbm.at[layer, e], stage.at[slot], sem_ref.at[slot])
                for hbm, stage in ((wg_hbm, wg_f), (wu_hbm, wu_f), (wd_hbm, wd_f))]

    @pl.when(w < nw_ref[0])
    def _():
        e = we_ref[w]
        b = wb_ref[w]
        slot = slot_ref[w]
        prev = jnp.maximum(w - 1, 0)
        new_expert = (w == 0) | (we_ref[prev] != e)
        new_block = (w == 0) | (wb_ref[prev] != b)

        @pl.when(w == 0)
        def _():
            for copy in weight_copies(e, slot):
                copy.start()

        @pl.when(new_expert)
        def _():
            for copy in weight_copies(e, slot):
                copy.wait()
            nxt = nxt_ref[w]

            @pl.when(nxt >= 0)
            def _():
                for copy in weight_copies(nxt, 1 - slot):
                    copy.start()

            for src, dst in ((wg_f, wg_s), (wu_f, wu_s), (wd_f, wd_s)):
                for r in range(0, dst.shape[0], cast_rows):
                    dst[r:r + cast_rows, :] = src[slot, r:r + cast_rows, :].astype(_BF16)

        tb = x_ref.shape[0]
        x = x_ref[...].astype(_BF16)
        gate = _mm(x, wg_s[...])
        up = _mm(x, wu_s[...])
        hid = gate * (1.0 / (1.0 + jnp.exp(-gate))) * up
        row = b * tb + lax.broadcasted_iota(jnp.int32, (tb, 1), 0)
        hid = jnp.where((row >= lo_ref[e]) & (row < hi_ref[e]), hid, 0.0).astype(_BF16)
        out = _mm(hid, wd_s[...])

        @pl.when(new_block)
        def _():
            o_ref[...] = out

        @pl.when(jnp.logical_not(new_block))
        def _():
            o_ref[...] += out


def _experts(plan, xs, w_gate, w_up, w_down, layer, tb):
    a, d = xs.shape
    f = w_gate.shape[3]
    wb, we, nxt, slot, lo, hi, nw = plan
    cast_rows = _tile(f, 256)
    vmem = (4 * _nbytes((tb, d), _F32) + 6 * _nbytes((d, f), _F32) + 3 * _nbytes((d, f), _BF16)
            + 6 * _nbytes((tb, f), _F32) + 3 * _nbytes((tb, d), _F32) + 2 * _nbytes((cast_rows, d), _F32))

    def row_block(w, wb, *_):
        return (wb[w], 0)

    return pl.pallas_call(
        functools.partial(_experts_kernel, layer=layer, cast_rows=cast_rows),
        out_shape=jax.ShapeDtypeStruct((a, d), _F32),
        grid_spec=pltpu.PrefetchScalarGridSpec(
            num_scalar_prefetch=7,
            grid=(wb.shape[0],),
            in_specs=[
                pl.BlockSpec((tb, d), row_block),
                pl.BlockSpec(memory_space=pl.ANY),
                pl.BlockSpec(memory_space=pl.ANY),
                pl.BlockSpec(memory_space=pl.ANY),
            ],
            out_specs=pl.BlockSpec((tb, d), row_block),
            scratch_shapes=[pltpu.VMEM((2, d, f), _F32), pltpu.VMEM((2, d, f), _F32), pltpu.VMEM((2, f, d), _F32),
                            pltpu.VMEM((d, f), _BF16), pltpu.VMEM((d, f), _BF16), pltpu.VMEM((f, d), _BF16),
                            pltpu.SemaphoreType.DMA((2,))],
        ),
        compiler_params=_params(("arbitrary",), vmem),
        name="moe_experts",
    )(wb, we, nxt, slot, lo, hi, nw, xs, w_gate, w_up, w_down)


def _combine_kernel(dest_ref, x_ref, gates_ref, gf_ref, ys_hbm, o_ref, buf_ref, sem_ref, *, final_norm):
    i = pl.program_id(0)
    n = pl.num_programs(0)
    tm = x_ref.shape[0]
    slot = i % 2

    sub = buf_ref.shape[3]
    d = x_ref.shape[1]

    def row_copy(s, k, grp, u, src):
        return pltpu.make_async_copy(ys_hbm.at[pl.ds(src, 1), :], buf_ref.at[s, k, grp, pl.ds(u, 1), :],
                                     sem_ref.at[s])

    def issue(tile, s):
        def body(grp, carry):
            a = TOP_K * (tile * tm + grp * sub)
            for u in range(sub):
                row_copy(s, 0, grp, u, dest_ref[a + TOP_K * u]).start(priority=0)
                row_copy(s, 1, grp, u, dest_ref[a + TOP_K * u + 1]).start(priority=1)
            return carry
        lax.fori_loop(0, tm // sub, body, 0)

    @pl.when(i == 0)
    def _():
        issue(0, 0)

    @pl.when(i + 1 < n)
    def _():
        issue(i + 1, 1 - slot)

    def drain(grp, carry):
        for _ in range(sub):
            row_copy(slot, 0, 0, 0, 0).wait()
            row_copy(slot, 1, 0, 0, 0).wait()
        return carry

    lax.fori_loop(0, tm // sub, drain, 0)

    gates = gates_ref[...]
    y = (gates[:, 0:1] * buf_ref[slot, 0].reshape(tm, d) + gates[:, 1:2] * buf_ref[slot, 1].reshape(tm, d))
    out = x_ref[...] + y
    if final_norm:
        out = _rmsnorm(out, gf_ref[...])
    o_ref[...] = out


def _combine(dest, x, gates, ys, gf, final_norm):
    s, d = x.shape
    tm = _tile(s, 512)
    vmem = 4 * _nbytes((tm, d), _F32) + 4 * _nbytes((tm, d), _F32) + 6 * _nbytes((tm, d), _F32)
    return pl.pallas_call(
        functools.partial(_combine_kernel, final_norm=final_norm),
        out_shape=jax.ShapeDtypeStruct((s, d), _F32),
        grid_spec=pltpu.PrefetchScalarGridSpec(
            num_scalar_prefetch=1,
            grid=(s // tm,),
            in_specs=[
                pl.BlockSpec((tm, d), lambda i, dest: (i, 0)),
                pl.BlockSpec((tm, gates.shape[1]), lambda i, dest: (i, 0)),
                pl.BlockSpec((1, d), lambda i, dest: (0, 0)),
                pl.BlockSpec(memory_space=pl.ANY),
            ],
            out_specs=pl.BlockSpec((tm, d), lambda i, dest: (i, 0)),
            scratch_shapes=[pltpu.VMEM((2, TOP_K, tm // V7X_F32_SUBLANES, V7X_F32_SUBLANES, d), _F32),
                            pltpu.SemaphoreType.DMA((2,))],
        ),
        compiler_params=_params(("arbitrary",), vmem),
        name="moe_combine",
    )(dest, x, gates, gf, ys)


def _moe_plan(ids, counts, n_groups, n_experts, n_rows, tb):
    cnt = counts[0, n_groups:n_groups + n_experts].astype(jnp.int32)
    ends = jnp.cumsum(cnt)
    starts = ends - cnt
    experts = jnp.arange(n_experts, dtype=jnp.int32)

    def lookup(table, idx):
        return jnp.sum(jnp.where(idx[..., None] == experts, table, 0), axis=-1)

    dest = (lookup(starts, ids[:, 0:TOP_K]) + ids[:, TOP_K:2 * TOP_K]).reshape(-1).astype(jnp.int32)

    first_blk = starts // tb
    n_items = jnp.where(cnt > 0, (ends - 1) // tb - first_blk + 1, 0)
    item_end = jnp.cumsum(n_items)
    n_work = item_end[-1]
    max_work = n_rows // tb + n_experts
    w = jnp.minimum(jnp.arange(max_work, dtype=jnp.int32), n_work - 1)
    we = jnp.sum((item_end[None, :] <= w[:, None]).astype(jnp.int32), axis=1)
    wb = (lookup(first_blk - (item_end - n_items), we) + w).astype(jnp.int32)
    later = (experts[None, :] > experts[:, None]) & (cnt[None, :] > 0)
    next_expert = jnp.min(jnp.where(later, experts[None, :], n_experts), axis=1)
    next_expert = jnp.where(next_expert == n_experts, -1, next_expert)
    stage_slot = (jnp.cumsum((cnt > 0).astype(jnp.int32)) - 1) % 2
    plan = (wb, we, lookup(next_expert, we).astype(jnp.int32), lookup(stage_slot, we).astype(jnp.int32),
            starts.astype(jnp.int32), ends.astype(jnp.int32), n_work.reshape(1).astype(jnp.int32))
    return dest, plan


def _hierarchical_moe(x, g, rg_w, rg_b, re_w, re_b, w_gate, w_up, w_down, layer, gf, final_norm):
    s, d = x.shape
    n_groups, _, epg = re_w.shape
    n_experts = n_groups * epg
    lanes = V7X_LANES
    assert n_groups + n_experts <= lanes
    wr = jnp.concatenate([rg_w, jnp.transpose(re_w, (1, 0, 2)).reshape(d, n_experts)], axis=1)
    wr = jnp.pad(wr, ((0, 0), (0, lanes - wr.shape[1])))
    br = jnp.pad(jnp.concatenate([rg_b, re_b.reshape(-1)]), (0, lanes - n_groups - n_experts)).reshape(1, lanes)

    ids, gates, counts = _router(x, g, wr, br, n_groups, epg)
    tb = _tile(TOP_K * s, 256)
    dest, plan = _moe_plan(ids, counts, n_groups, n_experts, TOP_K * s, tb)
    xs = _dispatch(dest, x, g)
    ys = _experts(plan, xs, w_gate, w_up, w_down, layer, tb)
    return _combine(dest, x, gates, ys, gf, final_norm)


def kernel(x, mix_norm, ffn_norm, final_norm, conv_w_in, conv_w, conv_w_out, attn_w_in, attn_b_f, attn_w_out,
           router_group_w, router_group_b, router_expert_w, router_expert_b, w_gate, w_up, w_down):
    b, s, d = x.shape
    assert b == 1 and mix_norm.shape[0] == 2
    n_heads = attn_b_f.shape[-1]
    lanes = V7X_LANES
    gf = final_norm.reshape(1, d)
    h = x.reshape(s, d)

    go, z = _conv_inproj(h, mix_norm[0].reshape(1, d), conv_w_in[0])
    h = _conv_outproj(go, z, conv_w[0], conv_w_out[0], h)
    h = _hierarchical_moe(h, ffn_norm[0].reshape(1, d), router_group_w[0], router_group_b[0],
                          router_expert_w[0], router_expert_b[0], w_gate, w_up, w_down, 0,
                          gf, final_norm=False)

    wf = jnp.pad(attn_w_in[0][:, 3 * d:], ((0, 0), (0, lanes - n_heads)))
    bf = jnp.pad(attn_b_f[0], (0, lanes - n_heads)).reshape(1, lanes)
    qkv, fl = _attn_inproj(h, mix_norm[1].reshape(1, d), attn_w_in[0], wf, bf, n_heads)
    kb = _forget_cumsum(fl, n_heads)
    o = _fox_attention(qkv, kb, n_heads, d)
    h = _matmul_residual(o, attn_w_out[0], h)
    h = _hierarchical_moe(h, ffn_norm[1].reshape(1, d), router_group_w[1], router_group_b[1],
                          router_expert_w[1], router_expert_b[1], w_gate, w_up, w_down, 1,
                          gf, final_norm=True)
    return h.reshape(b, s, d)
```

```python
import functools
import math

import jax
import jax.numpy as jnp
from jax import lax
from jax.experimental import pallas as pl
from jax.experimental.pallas import tpu as pltpu

RMS_EPS = 1e-6
NEG_INF = -1e30
TOP_K = 2
LOG2E = 1.4426950408889634

V7X_LANES = 128
V7X_F32_SUBLANES = 8
V7X_BF16_SUBLANES = 16
V7X_VMEM_BYTES = 64 * 1024 * 1024

_BF16 = jnp.bfloat16
_F32 = jnp.float32


def _tile(n, pref):
    t = min(n, pref)
    while n % t:
        t //= 2
    return t


def _params(semantics, vmem_bytes):
    limit = min(int(vmem_bytes), V7X_VMEM_BYTES - 4 * 1024 * 1024)
    return pltpu.CompilerParams(dimension_semantics=semantics, vmem_limit_bytes=limit)


def _nbytes(shape, dtype):
    return math.prod(shape) * jnp.dtype(dtype).itemsize


def _rmsnorm(x, g):
    ms = jnp.mean(x * x, axis=-1, keepdims=True)
    return x * lax.rsqrt(ms + RMS_EPS) * g


def _mm(a, b):
    return jnp.dot(a, b, preferred_element_type=_F32)


def _conv_inproj_kernel(x_ref, g_ref, wgo_ref, wgi_ref, wu_ref, go_ref, z_ref, hn_ref):
    @pl.when(pl.program_id(1) == 0)
    def _():
        hn_ref[...] = _rmsnorm(x_ref[...], g_ref[...]).astype(_BF16)

    hn = hn_ref[...]
    go_ref[...] = _mm(hn, wgo_ref[...].astype(_BF16)).astype(go_ref.dtype)
    gate_in = _mm(hn, wgi_ref[...].astype(_BF16))
    u = _mm(hn, wu_ref[...].astype(_BF16))
    z_ref[...] = (gate_in * u).astype(z_ref.dtype)


def _conv_inproj(x, g, w_in):
    s, d = x.shape
    tm, tn = _tile(s, 1024), _tile(d, 256)
    nj = d // tn
    vmem = (2 * _nbytes((tm, d), _F32) + _nbytes((tm, d), _BF16) + 6 * _nbytes((d, tn), _F32)
            + 3 * _nbytes((d, tn), _BF16) + 4 * _nbytes((tm, tn), _BF16) + 4 * _nbytes((tm, tn), _F32)
            + 2 * _nbytes((tm, d), _F32))
    return pl.pallas_call(
        _conv_inproj_kernel,
        out_shape=(jax.ShapeDtypeStruct((s, d), _BF16), jax.ShapeDtypeStruct((s, d), _BF16)),
        grid=(s // tm, nj),
        in_specs=[
            pl.BlockSpec((tm, d), lambda i, j: (i, 0)),
            pl.BlockSpec((1, d), lambda i, j: (0, 0)),
            pl.BlockSpec((d, tn), lambda i, j: (0, j)),
            pl.BlockSpec((d, tn), lambda i, j: (0, j + nj)),
            pl.BlockSpec((d, tn), lambda i, j: (0, j + 2 * nj)),
        ],
        out_specs=(pl.BlockSpec((tm, tn), lambda i, j: (i, j)),
                   pl.BlockSpec((tm, tn), lambda i, j: (i, j))),
        scratch_shapes=[pltpu.VMEM((tm, d), _BF16)],
        compiler_params=_params(("parallel", "arbitrary"), vmem),
        name="conv_inproj",
    )(x, g, w_in, w_in, w_in)


def _conv_outproj_kernel(go_ref, z_ref, halo_ref, cw_ref, w_ref, x_ref, o_ref, y_ref, *, chunk):
    i = pl.program_id(0)

    @pl.when(pl.program_id(1) == 0)
    def _():
        tm, d = y_ref.shape
        row = lax.broadcasted_iota(jnp.int32, (tm, chunk), 0)
        keep = (i > 0).astype(_F32)
        last = halo_ref.shape[0] - 1
        for c in range(d // chunk):
            cs = slice(c * chunk, (c + 1) * chunk)
            z = z_ref[:, cs].astype(_F32)
            halo = halo_ref[:, cs].astype(_F32) * keep
            h1 = halo[last:last + 1, :]
            h2 = halo[last - 1:last, :]
            z1 = jnp.where(row == 0, h1, pltpu.roll(z, 1, axis=0))
            z2 = jnp.where(row == 0, h2, jnp.where(row == 1, h1, pltpu.roll(z, 2, axis=0)))
            cw = cw_ref[:, cs]
            conv = cw[0:1, :] * z2 + cw[1:2, :] * z1 + cw[2:3, :] * z
            y_ref[:, cs] = (go_ref[:, cs].astype(_F32) * conv).astype(_BF16)

    o_ref[...] = x_ref[...] + _mm(y_ref[...], w_ref[...].astype(_BF16))


def _conv_outproj(go, z, conv_w, w_out, x):
    s, d = x.shape
    tm, tn = _tile(s, 1024), _tile(d, 512)
    hb = V7X_BF16_SUBLANES
    chunk = _tile(d, 512)
    vmem = (4 * _nbytes((tm, d), _BF16) + _nbytes((tm, d), _BF16) + 2 * _nbytes((d, tn), _F32)
            + _nbytes((d, tn), _BF16) + 6 * _nbytes((tm, tn), _F32) + 8 * _nbytes((tm, chunk), _F32))
    return pl.pallas_call(
        functools.partial(_conv_outproj_kernel, chunk=chunk),
        out_shape=jax.ShapeDtypeStruct((s, d), _F32),
        grid=(s // tm, d // tn),
        in_specs=[
            pl.BlockSpec((tm, d), lambda i, j: (i, 0)),
            pl.BlockSpec((tm, d), lambda i, j: (i, 0)),
            pl.BlockSpec((hb, d), lambda i, j: (jnp.maximum(i * (tm // hb) - 1, 0), 0)),
            pl.BlockSpec((conv_w.shape[0], d), lambda i, j: (0, 0)),
            pl.BlockSpec((d, tn), lambda i, j: (0, j)),
            pl.BlockSpec((tm, tn), lambda i, j: (i, j)),
        ],
        out_specs=pl.BlockSpec((tm, tn), lambda i, j: (i, j)),
        scratch_shapes=[pltpu.VMEM((tm, d), _BF16)],
        compiler_params=_params(("parallel", "arbitrary"), vmem),
        name="conv_outproj",
    )(go, z, z, conv_w, w_out, x)


def _attn_inproj_kernel(x_ref, g_ref, w_ref, wf_ref, bf_ref, qkv_ref, fl_ref, hn_ref, *, n_q_blocks, q_scale):
    j = pl.program_id(1)

    @pl.when(j == 0)
    def _():
        hn = _rmsnorm(x_ref[...], g_ref[...]).astype(_BF16)
        hn_ref[...] = hn
        fl_ref[...] = _mm(hn, wf_ref[...].astype(_BF16)) + bf_ref[...]

    res = _mm(hn_ref[...], w_ref[...].astype(_BF16)) * jnp.where(j < n_q_blocks, q_scale, 1.0)
    res = res.astype(qkv_ref.dtype)
    hd = qkv_ref.shape[2]
    for h in range(qkv_ref.shape[0]):
        qkv_ref[h] = res[:, h * hd:(h + 1) * hd]


def _attn_inproj(x, g, w_in, wf, bf, n_heads):
    s, d = x.shape
    hd = d // n_heads
    tm, tn = _tile(s, 1024), _tile(d, 1024)
    hpb = tn // hd
    kern = functools.partial(_attn_inproj_kernel, n_q_blocks=d // tn, q_scale=LOG2E / math.sqrt(hd))
    vmem = (2 * _nbytes((tm, d), _F32) + _nbytes((tm, d), _BF16) + 2 * _nbytes((d, tn), _F32)
            + _nbytes((d, tn), _BF16) + 3 * _nbytes((tm, tn), _F32) + 2 * _nbytes((tm, d), _F32)
            + 4 * _nbytes((d, V7X_LANES), _F32))
    return pl.pallas_call(
        kern,
        out_shape=(jax.ShapeDtypeStruct((3 * n_heads, s, hd), _BF16),
                   jax.ShapeDtypeStruct((s, V7X_LANES), _F32)),
        grid=(s // tm, 3 * d // tn),
        in_specs=[
            pl.BlockSpec((tm, d), lambda i, j: (i, 0)),
            pl.BlockSpec((1, d), lambda i, j: (0, 0)),
            pl.BlockSpec((d, tn), lambda i, j: (0, j)),
            pl.BlockSpec((d, V7X_LANES), lambda i, j: (0, 0)),
            pl.BlockSpec((1, V7X_LANES), lambda i, j: (0, 0)),
        ],
        out_specs=(pl.BlockSpec((hpb, tm, hd), lambda i, j: (j, i, 0)),
                   pl.BlockSpec((tm, V7X_LANES), lambda i, j: (i, 0))),
        scratch_shapes=[pltpu.VMEM((tm, d), _BF16)],
        compiler_params=_params(("parallel", "arbitrary"), vmem),
        name="attn_inproj",
    )(x, g, w_in, wf, bf)


def _forget_cumsum_kernel(fl_ref, kb_ref, carry_ref, *, n_heads):
    @pl.when(pl.program_id(0) == 0)
    def _():
        carry_ref[...] = jnp.zeros_like(carry_ref)

    fl = fl_ref[...]
    ls = jnp.minimum(fl, 0.0) - jnp.log1p(jnp.exp(-jnp.abs(fl)))
    tc = ls.shape[0]
    row = lax.broadcasted_iota(jnp.int32, ls.shape, 0)
    shift = 1
    while shift < tc:
        ls = jnp.where(row >= shift, ls + pltpu.roll(ls, shift, axis=0), ls)
        shift *= 2
    ls = ls + carry_ref[...]
    carry_ref[...] = ls[tc - 1:tc, :]

    lane = lax.broadcasted_iota(jnp.int32, ls.shape, 1)
    bias = jnp.where(lane < n_heads, ls * (-LOG2E), 0.0)
    hi = bias.astype(_BF16).astype(_F32)
    rem = bias - hi
    mid = rem.astype(_BF16).astype(_F32)
    lo = (rem - mid).astype(_BF16).astype(_F32)
    pieces = hi + pltpu.roll(mid, n_heads, axis=1) + pltpu.roll(lo, 2 * n_heads, axis=1)
    kb_ref[...] = pieces.astype(_BF16)


def _forget_cumsum(fl, n_heads):
    s, lanes = fl.shape
    assert 3 * n_heads <= lanes
    tc = _tile(s, 2048)
    return pl.pallas_call(
        functools.partial(_forget_cumsum_kernel, n_heads=n_heads),
        out_shape=jax.ShapeDtypeStruct((s, lanes), _BF16),
        grid=(s // tc,),
        in_specs=[pl.BlockSpec((tc, lanes), lambda i: (i, 0))],
        out_specs=pl.BlockSpec((tc, lanes), lambda i: (i, 0)),
        scratch_shapes=[pltpu.VMEM((1, lanes), _F32)],
        compiler_params=_params(("arbitrary",), 16 * _nbytes((tc, lanes), _F32)),
        name="forget_cumsum",
    )(fl)


NEXT_QUERY_BLOCK = "next query block"


def _fox_attn_kernel(q_ref, k_ref, v_ref, kb_ref, o_ref, m_ref, acc_ref, s_ref, p_ref, *, n_heads, tq, tk, group):
    h = pl.program_id(0)
    i = pl.program_id(1)
    hd = q_ref.shape[1]
    per_q = tq // tk
    lane = lax.broadcasted_iota(jnp.int32, (tq, hd), 1)
    head_sel = jnp.where((lane == h) | (lane == n_heads + h) | (lane == 2 * n_heads + h), 1.0, 0.0).astype(_BF16)
    ones_col = jnp.where(lax.broadcasted_iota(jnp.int32, (tk, hd), 1) == 0, 1.0, 0.0).astype(_BF16)

    def query_block(idx):
        q0 = pl.multiple_of(idx * tq, tq)
        return jnp.concatenate([q_ref[pl.ds(q0, tq), :], head_sel], axis=1)

    q_ext = query_block(i)

    def scores(j, row0=0, q=None):
        k0 = pl.multiple_of(j * tk, tk)
        k_ext = jnp.concatenate([k_ref[pl.ds(k0, tk), :], kb_ref[pl.ds(k0, tk), :]], axis=1)
        q = q_ext if q is None else q
        return lax.dot_general(q[row0:], k_ext, (((1,), (1,)), ((), ())), preferred_element_type=_F32)

    def weighted_values(p, j):
        k0 = pl.multiple_of(j * tk, tk)
        return _mm(p, jnp.concatenate([v_ref[pl.ds(k0, tk), :], ones_col], axis=1))

    def probs(diag, s, m_prev):
        if diag is not None:
            qi = lax.broadcasted_iota(jnp.int32, s.shape, 0)
            ki = lax.broadcasted_iota(jnp.int32, s.shape, 1)
            s = jnp.where(ki <= qi, s, NEG_INF)
        chunks = [s[:, c * hd:(c + 1) * hd] for c in range(tk // hd)]
        m_new = jnp.maximum(m_prev, jnp.max(functools.reduce(jnp.maximum, chunks), axis=-1, keepdims=True))
        alpha = jnp.exp2(m_prev - m_new)
        p = jnp.concatenate([jnp.exp2(c - m_new).astype(_BF16) for c in chunks], axis=1)
        return m_new, jnp.concatenate([alpha, alpha], axis=1), p

    def run(blocks, prefetch, pending, defer):
        m, acc = m_ref[...], acc_ref[...]
        if pending:
            acc = acc + weighted_values(p_ref[...], blocks[0][0] - 1)
        for n, (j, diag) in enumerate(blocks):
            row0 = (diag or 0) * tk
            s = s_ref[...] if n == 0 else scores(j, row0)
            m_low, alpha, p = probs(diag, s, m[row0:])
            if defer and n == len(blocks) - 1:
                acc = alpha * acc
                p_ref[...] = p
            else:
                acc_low = alpha * acc[row0:] + weighted_values(p, j)
                acc = jnp.concatenate([acc[:row0], acc_low], axis=0) if row0 else acc_low
            m = jnp.concatenate([m[:row0], m_low], axis=0) if row0 else m_low
        if prefetch is NEXT_QUERY_BLOCK:
            s_ref[...] = scores(0, q=query_block(jnp.minimum(i + 1, pl.num_programs(1) - 1)))
        elif prefetch is not None:
            s_ref[...] = scores(prefetch)
        m_ref[...] = m
        acc_ref[...] = acc

    m_ref[...] = jnp.full_like(m_ref, -jnp.inf)
    acc_ref[...] = jnp.zeros_like(acc_ref)

    @pl.when(i == 0)
    def _():
        s_ref[...] = scores(0)
    n_unmasked = per_q * i
    n_main = n_unmasked // group
    left_over = n_unmasked - group * n_main

    def main_group(t):
        return [(group * t + u, None) for u in range(group)]

    @pl.when(n_main > 0)
    def _():
        run(main_group(0), group, pending=False, defer=True)

    def body(t, carry):
        run(main_group(t), group * (t + 1), pending=True, defer=True)
        return carry

    lax.fori_loop(1, n_main, body, 0)
    diagonal = [(n_unmasked + u, u) for u in range(per_q)]
    for extra in range(0, group, per_q):
        blocks = [(n_unmasked - extra + u, None) for u in range(extra)] + diagonal
        for pending in (False, True):
            @pl.when((left_over == extra) & ((n_main > 0) == pending))
            def _(blocks=blocks, pending=pending):
                run(blocks, NEXT_QUERY_BLOCK, pending=pending, defer=False)

    acc = acc_ref[...]
    o_ref[...] = (acc[:, :hd] / acc[:, hd:hd + 1]).astype(o_ref.dtype)


def _fox_attention(qkv, kb, n_heads, d):
    _, s, hd = qkv.shape
    assert kb.shape == (s, hd)
    tq = _tile(s, 1024)
    tk = _tile(tq, 512)
    group = 8
    assert group % (tq // tk) == 0
    vmem = (12 * _nbytes((s, hd), _BF16) + 12 * _nbytes((tq, 2 * hd), _F32) + 3 * group * _nbytes((tq, tk), _F32))
    return pl.pallas_call(
        functools.partial(_fox_attn_kernel, n_heads=n_heads, tq=tq, tk=tk, group=group),
        out_shape=jax.ShapeDtypeStruct((s, d), _BF16),
        grid=(n_heads, s // tq),
        in_specs=[
            pl.BlockSpec((None, s, hd), lambda h, i: (h, 0, 0)),
            pl.BlockSpec((None, s, hd), lambda h, i: (n_heads + h, 0, 0)),
            pl.BlockSpec((None, s, hd), lambda h, i: (2 * n_heads + h, 0, 0)),
            pl.BlockSpec((s, hd), lambda h, i: (0, 0)),
        ],
        out_specs=pl.BlockSpec((tq, hd), lambda h, i: (i, h)),
        scratch_shapes=[pltpu.VMEM((tq, hd), _F32), pltpu.VMEM((tq, 2 * hd), _F32), pltpu.VMEM((tq, tk), _F32),
                        pltpu.VMEM((tq, tk), _BF16)],
        compiler_params=_params(("parallel", "arbitrary"), vmem),
        name="fox_attention",
    )(qkv, qkv, qkv, kb)


def _matmul_residual_kernel(a_ref, w_ref, x_ref, o_ref):
    o_ref[...] = x_ref[...] + _mm(a_ref[...], w_ref[...].astype(_BF16))


def _matmul_residual(a, w, x):
    s, d = x.shape
    tm, tn = _tile(s, 1024), _tile(d, 512)
    vmem = (2 * _nbytes((tm, d), _BF16) + 2 * _nbytes((d, tn), _F32) + _nbytes((d, tn), _BF16)
            + 6 * _nbytes((tm, tn), _F32))
    return pl.pallas_call(
        _matmul_residual_kernel,
        out_shape=jax.ShapeDtypeStruct((s, d), _F32),
        grid=(s // tm, d // tn),
        in_specs=[
            pl.BlockSpec((tm, d), lambda i, j: (i, 0)),
            pl.BlockSpec((d, tn), lambda i, j: (0, j)),
            pl.BlockSpec((tm, tn), lambda i, j: (i, j)),
        ],
        out_specs=pl.BlockSpec((tm, tn), lambda i, j: (i, j)),
        compiler_params=_params(("parallel", "arbitrary"), vmem),
        name="attn_outproj",
    )(a, w, x)


def _router_kernel(x_ref, g_ref, wr_ref, br_ref, ids_ref, gates_ref, counts_ref, tril_ref, carry_ref,
                   *, n_groups, epg):
    i = pl.program_id(0)
    tm = x_ref.shape[0]
    lanes = wr_ref.shape[1]

    @pl.when(i == 0)
    def _():
        r = lax.broadcasted_iota(jnp.int32, (tm, tm), 0)
        c = lax.broadcasted_iota(jnp.int32, (tm, tm), 1)
        tril_ref[...] = (c < r).astype(_BF16)
        carry_ref[...] = jnp.zeros_like(carry_ref)

    hn = _rmsnorm(x_ref[...], g_ref[...])
    logits = _mm(hn.astype(_BF16), wr_ref[...].astype(_BF16)) + br_ref[...]
    lane = lax.broadcasted_iota(jnp.int32, (tm, lanes), 1)

    def first_argmax(vals):
        top = jnp.max(vals, axis=-1, keepdims=True)
        idx = jnp.min(jnp.where(vals == top, lane, lanes), axis=-1, keepdims=True)
        return top, idx

    gl = jnp.where(lane < n_groups, logits, -jnp.inf)
    g_top, g_sel = first_argmax(gl)
    p_g = 1.0 / jnp.sum(jnp.exp(gl - g_top), axis=-1, keepdims=True)
    lo = n_groups + g_sel * epg
    el = jnp.where((lane >= lo) & (lane < lo + epg), logits, -jnp.inf)
    v1, i1 = first_argmax(el)
    v2, i2 = first_argmax(jnp.where(lane == i1, -jnp.inf, el))
    t = jnp.exp(v2 - v1)
    den = 1.0 + t
    gate1 = (1.0 / den) * p_g
    gate2 = (t / den) * p_g

    sel1 = lane == i1
    sel2 = lane == i2
    onehot = (sel1 | sel2).astype(_F32)
    prefix = _mm(tril_ref[...], onehot.astype(_BF16)) + carry_ref[...]
    rank1 = jnp.sum(jnp.where(sel1, prefix, 0.0), axis=-1, keepdims=True).astype(jnp.int32)
    rank2 = jnp.sum(jnp.where(sel2, prefix, 0.0), axis=-1, keepdims=True).astype(jnp.int32)
    carry = carry_ref[...] + jnp.sum(onehot, axis=0, keepdims=True)
    carry_ref[...] = carry
    counts_ref[...] = carry

    ids = jnp.where(lane == 0, i1 - n_groups,
                    jnp.where(lane == 1, i2 - n_groups,
                              jnp.where(lane == 2, rank1, jnp.where(lane == 3, rank2, 0))))
    gates = jnp.where(lane == 0, gate1, jnp.where(lane == 1, gate2, 0.0))
    ids_ref[...] = ids[:, :ids_ref.shape[1]]
    gates_ref[...] = gates[:, :gates_ref.shape[1]]


def _router(x, g, wr, br, n_groups, epg):
    s, d = x.shape
    lanes = wr.shape[1]
    tm = _tile(s, 512)
    vmem = (4 * _nbytes((tm, d), _F32) + _nbytes((tm, tm), _BF16) + 4 * _nbytes((d, lanes), _F32)
            + 24 * _nbytes((tm, lanes), _F32) + 4 * _nbytes((tm, tm), jnp.int32))
    return pl.pallas_call(
        functools.partial(_router_kernel, n_groups=n_groups, epg=epg),
        out_shape=(jax.ShapeDtypeStruct((s, 8), jnp.int32), jax.ShapeDtypeStruct((s, 8), _F32),
                   jax.ShapeDtypeStruct((1, lanes), _F32)),
        grid=(s // tm,),
        in_specs=[
            pl.BlockSpec((tm, d), lambda i: (i, 0)),
            pl.BlockSpec((1, d), lambda i: (0, 0)),
            pl.BlockSpec((d, lanes), lambda i: (0, 0)),
            pl.BlockSpec((1, lanes), lambda i: (0, 0)),
        ],
        out_specs=(pl.BlockSpec((tm, 8), lambda i: (i, 0)), pl.BlockSpec((tm, 8), lambda i: (i, 0)),
                   pl.BlockSpec((1, lanes), lambda i: (0, 0))),
        scratch_shapes=[pltpu.VMEM((tm, tm), _BF16), pltpu.VMEM((1, lanes), _F32)],
        compiler_params=_params(("arbitrary",), vmem),
        name="moe_router",
    )(x, g, wr, br)


def _dispatch_kernel(dest_ref, x_ref, g_ref, xs_hbm, hn_ref, sem_ref):
    i = pl.program_id(0)
    n = pl.num_programs(0)
    tm = x_ref.shape[0]
    slot = i % 2

    sub = hn_ref.shape[2]

    def row_copy(s, grp, u, dst):
        return pltpu.make_async_copy(hn_ref.at[s, grp, pl.ds(u, 1), :], xs_hbm.at[pl.ds(dst, 1), :], sem_ref.at[s])

    def drain(s):
        def body(grp, carry):
            for _ in range(TOP_K * sub):
                row_copy(s, 0, 0, 0).wait()
            return carry
        lax.fori_loop(0, tm // sub, body, 0)

    hn_ref[slot] = _rmsnorm(x_ref[...], g_ref[...]).reshape(tm // sub, sub, x_ref.shape[1])

    def issue(grp, carry):
        a = TOP_K * (i * tm + grp * sub)
        for u in range(sub):
            row_copy(slot, grp, u, dest_ref[a + TOP_K * u]).start(priority=0)
            row_copy(slot, grp, u, dest_ref[a + TOP_K * u + 1]).start(priority=1)
        return carry

    lax.fori_loop(0, tm // sub, issue, 0)

    @pl.when(i > 0)
    def _():
        drain(1 - slot)

    @pl.when(i == n - 1)
    def _():
        drain(slot)


def _dispatch(dest, x, g):
    s, d = x.shape
    tm = _tile(s, 512)
    vmem = 6 * _nbytes((tm, d), _F32)
    return pl.pallas_call(
        _dispatch_kernel,
        out_shape=jax.ShapeDtypeStruct((TOP_K * s, d), _F32),
        grid_spec=pltpu.PrefetchScalarGridSpec(
            num_scalar_prefetch=1,
            grid=(s // tm,),
            in_specs=[pl.BlockSpec((tm, d), lambda i, dest: (i, 0)),
                      pl.BlockSpec((1, d), lambda i, dest: (0, 0))],
            out_specs=pl.BlockSpec(memory_space=pl.ANY),
            scratch_shapes=[pltpu.VMEM((2, tm // V7X_F32_SUBLANES, V7X_F32_SUBLANES, d), _F32),
                            pltpu.SemaphoreType.DMA((2,))],
        ),
        compiler_params=_params(("arbitrary",), vmem),
        name="moe_dispatch",
    )(dest, x, g)


def _experts_kernel(wb_ref, we_ref, nxt_ref, slot_ref, lo_ref, hi_ref, nw_ref, x_ref, wg_hbm, wu_hbm, wd_hbm, o_ref,
                    wg_f, wu_f, wd_f, wg_s, wu_s, wd_s, sem_ref, *, layer, cast_rows):
    w = pl.program_id(0)

    def weight_copies(e, slot):
        return [pltpu.make_async_copy(hbm.at[layer, e], stage.at[slot], sem_ref.at[slot])
                for hbm, stage in ((wg_hbm, wg_f), (wu_hbm, wu_f), (wd_hbm, wd_f))]

    @pl.when(w < nw_ref[0])
    def _():
        e = we_ref[w]
        b = wb_ref[w]
        slot = slot_ref[w]
        prev = jnp.maximum(w - 1, 0)
        new_expert = (w == 0) | (we_ref[prev] != e)
        new_block = (w == 0) | (wb_ref[prev] != b)

        @pl.when(w == 0)
        def _():
            for copy in weight_copies(e, slot):
                copy.start()

        @pl.when(new_expert)
        def _():
            for copy in weight_copies(e, slot):
                copy.wait()
            nxt = nxt_ref[w]

            @pl.when(nxt >= 0)
            def _():
                for copy in weight_copies(nxt, 1 - slot):
                    copy.start()

            for src, dst in ((wg_f, wg_s), (wu_f, wu_s), (wd_f, wd_s)):
                for r in range(0, dst.shape[0], cast_rows):
                    dst[r:r + cast_rows, :] = src[slot, r:r + cast_rows, :].astype(_BF16)

        tb = x_ref.shape[0]
        x = x_ref[...].astype(_BF16)
        gate = _mm(x, wg_s[...])
        up = _mm(x, wu_s[...])
        hid = gate * (1.0 / (1.0 + jnp.exp(-gate))) * up
        row = b * tb + lax.broadcasted_iota(jnp.int32, (tb, 1), 0)
        hid = jnp.where((row >= lo_ref[e]) & (row < hi_ref[e]), hid, 0.0).astype(_BF16)

        @pl.when(new_block)
        def _():
            o_ref[...] = _mm(hid, wd_s[...])

        @pl.when(jnp.logical_not(new_block))
        def _():
            o_ref[...] += _mm(hid, wd_s[...])


def _experts(plan, xs, w_gate, w_up, w_down, layer, tb):
    a, d = xs.shape
    f = w_gate.shape[3]
    wb, we, nxt, slot, lo, hi, nw = plan
    cast_rows = _tile(f, 256)
    vmem = (4 * _nbytes((tb, d), _F32) + 6 * _nbytes((d, f), _F32) + 3 * _nbytes((d, f), _BF16)
            + 6 * _nbytes((tb, f), _F32) + 3 * _nbytes((tb, d), _F32) + 2 * _nbytes((cast_rows, d), _F32))

    def row_block(w, wb, *_):
        return (wb[w], 0)

    return pl.pallas_call(
        functools.partial(_experts_kernel, layer=layer, cast_rows=cast_rows),
        out_shape=jax.ShapeDtypeStruct((a, d), _F32),
        grid_spec=pltpu.PrefetchScalarGridSpec(
            num_scalar_prefetch=7,
            grid=(wb.shape[0],),
            in_specs=[
                pl.BlockSpec((tb, d), row_block),
                pl.BlockSpec(memory_space=pl.ANY),
                pl.BlockSpec(memory_space=pl.ANY),
                pl.BlockSpec(memory_space=pl.ANY),
            ],
            out_specs=pl.BlockSpec((tb, d), row_block),
            scratch_shapes=[pltpu.VMEM((2, d, f), _F32), pltpu.VMEM((2, d, f), _F32), pltpu.VMEM((2, f, d), _F32),
                            pltpu.VMEM((d, f), _BF16), pltpu.VMEM((d, f), _BF16), pltpu.VMEM((f, d), _BF16),
                            pltpu.SemaphoreType.DMA((2,))],
        ),
        compiler_params=_params(("arbitrary",), vmem),
        name="moe_experts",
    )(wb, we, nxt, slot, lo, hi, nw, xs, w_gate, w_up, w_down)


def _combine_kernel(dest_ref, x_ref, gates_ref, gf_ref, ys_hbm, o_ref, buf_ref, sem_ref, *, final_norm):
    i = pl.program_id(0)
    n = pl.num_programs(0)
    tm = x_ref.shape[0]
    slot = i % 2

    sub = buf_ref.shape[3]
    d = x_ref.shape[1]

    def row_copy(s, k, grp, u, src):
        return pltpu.make_async_copy(ys_hbm.at[pl.ds(src, 1), :], buf_ref.at[s, k, grp, pl.ds(u, 1), :],
                                     sem_ref.at[s])

    def issue(tile, s):
        def body(grp, carry):
            a = TOP_K * (tile * tm + grp * sub)
            for u in range(sub):
                row_copy(s, 0, grp, u, dest_ref[a + TOP_K * u]).start(priority=0)
                row_copy(s, 1, grp, u, dest_ref[a + TOP_K * u + 1]).start(priority=1)
            return carry
        lax.fori_loop(0, tm // sub, body, 0)

    @pl.when(i == 0)
    def _():
        issue(0, 0)

    @pl.when(i + 1 < n)
    def _():
        issue(i + 1, 1 - slot)

    def drain(grp, carry):
        for _ in range(sub):
            row_copy(slot, 0, 0, 0, 0).wait()
            row_copy(slot, 1, 0, 0, 0).wait()
        return carry

    lax.fori_loop(0, tm // sub, drain, 0)

    gates = gates_ref[...]
    y = (gates[:, 0:1] * buf_ref[slot, 0].reshape(tm, d) + gates[:, 1:2] * buf_ref[slot, 1].reshape(tm, d))
    out = x_ref[...] + y
    if final_norm:
        out = _rmsnorm(out, gf_ref[...])
    o_ref[...] = out


def _combine(dest, x, gates, ys, gf, final_norm):
    s, d = x.shape
    tm = _tile(s, 256)
    vmem = 4 * _nbytes((tm, d), _F32) + 4 * _nbytes((tm, d), _F32) + 6 * _nbytes((tm, d), _F32)
    return pl.pallas_call(
        functools.partial(_combine_kernel, final_norm=final_norm),
        out_shape=jax.ShapeDtypeStruct((s, d), _F32),
        grid_spec=pltpu.PrefetchScalarGridSpec(
            num_scalar_prefetch=1,
            grid=(s // tm,),
            in_specs=[
                pl.BlockSpec((tm, d), lambda i, dest: (i, 0)),
                pl.BlockSpec((tm, gates.shape[1]), lambda i, dest: (i, 0)),
                pl.BlockSpec((1, d), lambda i, dest: (0, 0)),
                pl.BlockSpec(memory_space=pl.ANY),
            ],
            out_specs=pl.BlockSpec((tm, d), lambda i, dest: (i, 0)),
            scratch_shapes=[pltpu.VMEM((2, TOP_K, tm // V7X_F32_SUBLANES, V7X_F32_SUBLANES, d), _F32),
                            pltpu.SemaphoreType.DMA((2,))],
        ),
        compiler_params=_params(("arbitrary",), vmem),
        name="moe_combine",
    )(dest, x, gates, gf, ys)


def _moe_plan(ids, counts, n_groups, n_experts, n_rows, tb):
    cnt = counts[0, n_groups:n_groups + n_experts].astype(jnp.int32)
    ends = jnp.cumsum(cnt)
    starts = ends - cnt
    experts = jnp.arange(n_experts, dtype=jnp.int32)

    def lookup(table, idx):
        return jnp.sum(jnp.where(idx[..., None] == experts, table, 0), axis=-1)

    dest = (lookup(starts, ids[:, 0:TOP_K]) + ids[:, TOP_K:2 * TOP_K]).reshape(-1).astype(jnp.int32)

    first_blk = starts // tb
    n_items = jnp.where(cnt > 0, (ends - 1) // tb - first_blk + 1, 0)
    item_end = jnp.cumsum(n_items)
    n_work = item_end[-1]
    max_work = n_rows // tb + n_experts
    w = jnp.minimum(jnp.arange(max_work, dtype=jnp.int32), n_work - 1)
    we = jnp.sum((item_end[None, :] <= w[:, None]).astype(jnp.int32), axis=1)
    wb = (lookup(first_blk - (item_end - n_items), we) + w).astype(jnp.int32)
    later = (experts[None, :] > experts[:, None]) & (cnt[None, :] > 0)
    next_expert = jnp.min(jnp.where(later, experts[None, :], n_experts), axis=1)
    next_expert = jnp.where(next_expert == n_experts, -1, next_expert)
    stage_slot = (jnp.cumsum((cnt > 0).astype(jnp.int32)) - 1) % 2
    plan = (wb, we, lookup(next_expert, we).astype(jnp.int32), lookup(stage_slot, we).astype(jnp.int32),
            starts.astype(jnp.int32), ends.astype(jnp.int32), n_work.reshape(1).astype(jnp.int32))
    return dest, plan


def _hierarchical_moe(x, g, rg_w, rg_b, re_w, re_b, w_gate, w_up, w_down, layer, gf, final_norm):
    s, d = x.shape
    n_groups, _, epg = re_w.shape
    n_experts = n_groups * epg
    lanes = V7X_LANES
    assert n_groups + n_experts <= lanes
    wr = jnp.concatenate([rg_w, jnp.transpose(re_w, (1, 0, 2)).reshape(d, n_experts)], axis=1)
    wr = jnp.pad(wr, ((0, 0), (0, lanes - wr.shape[1])))
    br = jnp.pad(jnp.concatenate([rg_b, re_b.reshape(-1)]), (0, lanes - n_groups - n_experts)).reshape(1, lanes)

    ids, gates, counts = _router(x, g, wr, br, n_groups, epg)
    tb = _tile(TOP_K * s, 256)
    dest, plan = _moe_plan(ids, counts, n_groups, n_experts, TOP_K * s, tb)
    xs = _dispatch(dest, x, g)
    ys = _experts(plan, xs, w_gate, w_up, w_down, layer, tb)
    return _combine(dest, x, gates, ys, gf, final_norm)


def kernel(x, mix_norm, ffn_norm, final_norm, conv_w_in, conv_w, conv_w_out, attn_w_in, attn_b_f, attn_w_out,
           router_group_w, router_group_b, router_expert_w, router_expert_b, w_gate, w_up, w_down):
    b, s, d = x.shape
    assert b == 1 and mix_norm.shape[0] == 2
    n_heads = attn_b_f.shape[-1]
    lanes = V7X_LANES
    gf = final_norm.reshape(1, d)
    h = x.reshape(s, d)

    go, z = _conv_inproj(h, mix_norm[0].reshape(1, d), conv_w_in[0])
    h = _conv_outproj(go, z, conv_w[0], conv_w_out[0], h)
    h = _hierarchical_moe(h, ffn_norm[0].reshape(1, d), router_group_w[0], router_group_b[0],
                          router_expert_w[0], router_expert_b[0], w_gate, w_up, w_down, 0,
                          gf, final_norm=False)

    wf = jnp.pad(attn_w_in[0][:, 3 * d:], ((0, 0), (0, lanes - n_heads)))
    bf = jnp.pad(attn_b_f[0], (0, lanes - n_heads)).reshape(1, lanes)
    qkv, fl = _attn_inproj(h, mix_norm[1].reshape(1, d), attn_w_in[0], wf, bf, n_heads)
    kb = _forget_cumsum(fl, n_heads)
    o = _fox_attention(qkv, kb, n_heads, d)
    h = _matmul_residual(o, attn_w_out[0], h)
    h = _hierarchical_moe(h, ffn_norm[1].reshape(1, d), router_group_w[1], router_group_b[1],
                          router_expert_w[1], router_expert_b[1], w_gate, w_up, w_down, 1,
                          gf, final_norm=True)
    return h.reshape(b, s, d)
```
